```python
import math
import jax, jax.numpy as jnp
from jax import lax
import numpy as np

D_MODEL = 1024
BATCH = 32
SEQ = 256
DEPTH = 2
DEC_BATCH = 8
DEC_SEQ = 2048
PAST_LEN = 512

GRID_W = 64
ROPE_THETA = 10000.0
NORM_EPS = 1e-6
BLOCK = 128
N_BRANCH = 4
BRANCH_W = D_MODEL // N_BRANCH
HEAD_DIM = 64
A_HEADS = BRANCH_W // HEAD_DIM
A_KV = A_HEADS // 2
B_HEADS = BRANCH_W // HEAD_DIM
B_KV = B_HEADS // 2
WINDOW = 128
SSM_INNER = BRANCH_W
SSM_HEAD_DIM = 64
SSM_HEADS = SSM_INNER // SSM_HEAD_DIM
SSM_GROUPS = 2
SSM_STATE = 64
SSM_CONV = 5
SSM_CHUNK = 128
CONV_CH = SSM_INNER + 2 * SSM_GROUPS * SSM_STATE
DIFF_HEADS = BRANCH_W // HEAD_DIM
DIFF_V_DIM = HEAD_DIM
DIFF_QK_DIM = HEAD_DIM // 2
N_EXPERTS = 16
EXPERT_FF = D_MODEL
EC_FACTOR = 2
PROJ_SIZES = (A_HEADS * HEAD_DIM, A_KV * HEAD_DIM, A_KV * HEAD_DIM,
              B_HEADS * HEAD_DIM, B_KV * HEAD_DIM, B_KV * HEAD_DIM,
              SSM_INNER, CONV_CH, 2 * SSM_HEADS,
              DIFF_HEADS * 2 * DIFF_QK_DIM, DIFF_HEADS * 2 * DIFF_QK_DIM, DIFF_HEADS * DIFF_V_DIM,
              N_BRANCH * D_MODEL)
N_IN = sum(PROJ_SIZES)

kernel_name = 'hybrid_diffusion_prefix_trunk_step'

F32 = jnp.float32


def rmsnorm(x, g):
    xf = x.astype(F32)
    y = xf * lax.rsqrt(jnp.mean(xf * xf, axis=-1, keepdims=True) + NORM_EPS)
    return (y * g.astype(F32)).astype(x.dtype)


def split_proj(p):
    idx = [int(i) for i in np.cumsum(PROJ_SIZES)[:-1]]
    return jnp.split(p, idx, axis=-1)


def axial_rope(x):
    S, d = x.shape[1], x.shape[-1]
    n = d // 4
    t = jnp.arange(S)
    rows = (t // GRID_W).astype(F32)
    cols = (t % GRID_W).astype(F32)
    inv = ROPE_THETA ** (-jnp.arange(n, dtype=F32) / n)
    ang = jnp.stack([rows[:, None] * inv, cols[:, None] * inv], axis=1)
    bshape = (1, S) + (1,) * (x.ndim - 3) + (2, n)
    cos = jnp.cos(ang).reshape(bshape)
    sin = jnp.sin(ang).reshape(bshape)
    xr = x.astype(F32).reshape(x.shape[:-1] + (2, 2, n))
    x1, x2 = xr[..., 0, :], xr[..., 1, :]
    out = jnp.stack([x1 * cos - x2 * sin, x2 * cos + x1 * sin], axis=-2)
    return out.reshape(x.shape).astype(x.dtype)


def dense_attention(q, k, v, sink=None):
    Bz, S, H, d = q.shape
    Hk = k.shape[2]
    G = H // Hk
    dv = v.shape[-1]
    nb = S // BLOCK
    scale = 1.0 / math.sqrt(d)
    qb = jnp.moveaxis(q.reshape(Bz, nb, BLOCK, Hk, G, d), 1, 0)

    def one(qblk):
        s = jnp.einsum('bqkgd,bskd->bkgqs', qblk, k).astype(F32) * scale
        if sink is not None:
            sk = jnp.broadcast_to(sink.astype(F32).reshape(1, Hk, G, 1, 1), s.shape[:-1] + (1,))
            s = jnp.concatenate([s, sk], axis=-1)
        p = jax.nn.softmax(s, axis=-1)
        if sink is not None:
            p = p[..., :-1]
        return jnp.einsum('bkgqs,bskd->bqkgd', p.astype(v.dtype), v)

    o = lax.map(one, qb)
    return jnp.moveaxis(o, 0, 1).reshape(Bz, S, H, dv)


def window_attention(q, k, v, kc, vc, sink):
    Bz, S, H, d = q.shape
    Hk = k.shape[2]
    G = H // Hk
    dv = v.shape[-1]
    nb = S // BLOCK
    W = 3 * BLOCK
    Lc = kc.shape[1]
    scale = 1.0 / math.sqrt(d)
    pad = ((0, 0), (BLOCK, BLOCK), (0, 0), (0, 0))
    kp = jnp.pad(k, pad).reshape(Bz, nb + 2, BLOCK, Hk, d)
    vp = jnp.pad(v, pad).reshape(Bz, nb + 2, BLOCK, Hk, dv)
    kw = jnp.concatenate([kp[:, :-2], kp[:, 1:-1], kp[:, 2:]], axis=2)
    vw = jnp.concatenate([vp[:, :-2], vp[:, 1:-1], vp[:, 2:]], axis=2)
    qb = q.reshape(Bz, nb, BLOCK, Hk, G, d)
    xs = (jnp.moveaxis(qb, 1, 0), jnp.moveaxis(kw, 1, 0), jnp.moveaxis(vw, 1, 0), jnp.arange(nb))

    def one(args):
        qblk, kblk, vblk, n = args
        qpos = n * BLOCK + jnp.arange(BLOCK)[:, None]
        kpos = (n - 1) * BLOCK + jnp.arange(W)[None, :]
        ok = (kpos >= 0) & (kpos < S) & (jnp.abs(qpos - kpos) <= WINDOW)
        s_loc = jnp.einsum('bqkgd,bskd->bkgqs', qblk, kblk).astype(F32) * scale
        s_loc = jnp.where(ok, s_loc, -jnp.inf)
        s_ctx = jnp.einsum('bqkgd,bskd->bkgqs', qblk, kc).astype(F32) * scale
        sk = jnp.broadcast_to(sink.astype(F32).reshape(1, Hk, G, 1, 1), s_loc.shape[:-1] + (1,))
        p = jax.nn.softmax(jnp.concatenate([s_loc, s_ctx, sk], axis=-1), axis=-1)
        o = jnp.einsum('bkgqs,bskd->bqkgd', p[..., :W].astype(v.dtype), vblk)
        return o + jnp.einsum('bkgqs,bskd->bqkgd', p[..., W:W + Lc].astype(v.dtype), vc)

    o = lax.map(one, xs)
    return jnp.moveaxis(o, 0, 1).reshape(Bz, S, H, dv)


def diff_attention(q, k, v, lam):
    Bz, S, H, _, dk = q.shape
    nb = S // BLOCK
    scale = 1.0 / math.sqrt(dk)
    qb = jnp.moveaxis(q.reshape(Bz, nb, BLOCK, H, 2, dk), 1, 0)

    def one(qblk):
        s = jnp.einsum('bqhmd,bshmd->bhmqs', qblk, k).astype(F32) * scale
        p = jax.nn.softmax(s, axis=-1)
        w = p[:, :, 0] - lam * p[:, :, 1]
        return jnp.einsum('bhqs,bshd->bqhd', w.astype(v.dtype), v)

    o = lax.map(one, qb)
    return jnp.moveaxis(o, 0, 1).reshape(Bz, S, H, v.shape[-1])


def dwconv_centred(u, w, b):
    C = u.shape[-1]
    K = w.shape[0]
    out = lax.conv_general_dilated(u, w[:, None, :].astype(u.dtype), window_strides=(1,),
                                   padding=[(K // 2, K // 2)],
                                   dimension_numbers=('NWC', 'WIO', 'NWC'),
                                   feature_group_count=C)
    return out + b.astype(u.dtype)


def ssd_chunked(x, la, Bm, Cm, h0):
    Bz, S, H, P = x.shape
    N = Bm.shape[-1]
    Q = SSM_CHUNK
    nc = S // Q
    xc = x.reshape(Bz, nc, Q, H, P)
    Bc = Bm.reshape(Bz, nc, Q, H, N)
    Cc = Cm.reshape(Bz, nc, Q, H, N)
    acs = jnp.cumsum(la.reshape(Bz, nc, Q, H), axis=2)
    causal = jnp.tril(jnp.ones((Q, Q), dtype=bool))
    seg = acs[:, :, :, None, :] - acs[:, :, None, :, :]
    decay = jnp.exp(jnp.where(causal[:, :, None], seg, -jnp.inf))
    scores = jnp.einsum('bclhn,bcshn->bclsh', Cc, Bc) * decay
    y_diag = jnp.einsum('bclsh,bcshp->bclhp', scores, xc)
    to_end = jnp.exp(acs[:, :, -1:, :] - acs)
    states = jnp.einsum('bclhn,bclh,bclhp->bchpn', Bc, to_end, xc)
    chunk_decay = jnp.exp(acs[:, :, -1, :])

    def step(hs, inp):
        s, dcy = inp
        return hs * dcy[:, :, None, None] + s, hs

    h_fin, h_in = lax.scan(step, h0.astype(F32),
                           (jnp.moveaxis(states, 1, 0), jnp.moveaxis(chunk_decay, 1, 0)))
    h_in = jnp.moveaxis(h_in, 0, 1)
    y_off = jnp.einsum('bclhn,bchpn->bclhp', Cc, h_in) * jnp.exp(acs)[..., None]
    return (y_diag + y_off).reshape(Bz, S, H, P), h_fin


def ssd_mixer(z, xbc, dt_raw, h0, lp):
    Bz, S, _ = xbc.shape
    u = jax.nn.silu(dwconv_centred(xbc, lp['conv_w'], lp['conv_b']))
    xs, Bm, Cm = jnp.split(u, [SSM_INNER, SSM_INNER + SSM_GROUPS * SSM_STATE], axis=-1)
    rep = SSM_HEADS // SSM_GROUPS
    xs = xs.reshape(Bz, S, SSM_HEADS, SSM_HEAD_DIM).astype(F32)
    Bm = jnp.repeat(Bm.reshape(Bz, S, SSM_GROUPS, SSM_STATE), rep, axis=2).astype(F32)
    Cm = jnp.repeat(Cm.reshape(Bz, S, SSM_GROUPS, SSM_STATE), rep, axis=2).astype(F32)
    dt = jax.nn.softplus(dt_raw.reshape(Bz, S, 2, SSM_HEADS).astype(F32) + lp['dt_bias'].astype(F32))
    A = -jnp.exp(lp['a_log'].astype(F32))
    xdt = xs[:, :, None] * dt[..., None]
    la = dt * A
    fl = lambda t: jnp.flip(t, axis=1)
    y_f, h_f = ssd_chunked(xdt[:, :, 0], la[:, :, 0], Bm, Cm, h0[:, 0])
    y_b, h_b = ssd_chunked(fl(xdt[:, :, 1]), fl(la[:, :, 1]), fl(Bm), fl(Cm), h0[:, 1])
    y = y_f + fl(y_b) + xs * lp['d_skip'].astype(F32)[:, None]
    y = y.reshape(Bz, S, SSM_INNER) * jax.nn.silu(z.astype(F32))
    return rmsnorm(y, lp['g_ssm']).astype(z.dtype), jnp.stack([h_f, h_b], axis=1)


def token_mixers(h, lp, li, ctx):
    Bz, S, _ = h.shape
    latent = ctx is not None
    proj = jnp.einsum('bsd,dn->bsn', h, lp['w_in'])
    (a_q, a_k, a_v, b_q, b_k, b_v, c_z, c_xbc, c_dt, d_q, d_k, d_v, gate) = split_proj(proj)
    qa = rmsnorm(a_q.reshape(Bz, S, A_HEADS, HEAD_DIM), lp['g_qa'])
    ka = rmsnorm(a_k.reshape(Bz, S, A_KV, HEAD_DIM), lp['g_ka'])
    va = a_v.reshape(Bz, S, A_KV, HEAD_DIM)
    qb = rmsnorm(b_q.reshape(Bz, S, B_HEADS, HEAD_DIM), lp['g_qb'])
    kb = rmsnorm(b_k.reshape(Bz, S, B_KV, HEAD_DIM), lp['g_kb'])
    vb = b_v.reshape(Bz, S, B_KV, HEAD_DIM)
    qd = rmsnorm(d_q.reshape(Bz, S, DIFF_HEADS, 2, DIFF_QK_DIM), lp['g_qd'])
    kd = rmsnorm(d_k.reshape(Bz, S, DIFF_HEADS, 2, DIFF_QK_DIM), lp['g_kd'])
    vd = d_v.reshape(Bz, S, DIFF_HEADS, DIFF_V_DIM)
    if latent:
        qa, ka, qb, kb, qd, kd = [axial_rope(t) for t in (qa, ka, qb, kb, qd, kd)]
        ya = dense_attention(qa, jnp.concatenate([ka, ctx['a_k'].astype(ka.dtype)], axis=1),
                             jnp.concatenate([va, ctx['a_v'].astype(va.dtype)], axis=1))
        yb = window_attention(qb, kb, vb, ctx['b_k'].astype(kb.dtype), ctx['b_v'].astype(vb.dtype),
                              lp['sink_b'])
        kd_all = jnp.concatenate([kd, ctx['d_k'].astype(kd.dtype)], axis=1)
        vd_all = jnp.concatenate([vd, ctx['d_v'].astype(vd.dtype)], axis=1)
        h0 = ctx['ssm']
    else:
        ya = dense_attention(qa, ka, va)
        yb = dense_attention(qb, kb, vb, sink=lp['sink_b'])
        kd_all, vd_all = kd, vd
        h0 = jnp.zeros((Bz, 2, SSM_HEADS, SSM_HEAD_DIM, SSM_STATE), F32)
    lam_init = 0.8 - 0.6 * math.exp(-0.3 * li)
    lam = (jnp.exp(jnp.sum(lp['lam_q1'].astype(F32) * lp['lam_k1'].astype(F32)))
           - jnp.exp(jnp.sum(lp['lam_q2'].astype(F32) * lp['lam_k2'].astype(F32))) + lam_init)
    yd = diff_attention(qd, kd_all, vd_all, lam)
    yd = rmsnorm(yd, lp['g_dout']) * (1.0 - lam_init)
    yc, h_fin = ssd_mixer(c_z, c_xbc, c_dt, h0, lp)
    branches = jnp.stack([ya.reshape(Bz, S, BRANCH_W), yb.reshape(Bz, S, BRANCH_W),
                          yc, yd.reshape(Bz, S, BRANCH_W)], axis=2)
    up = jnp.einsum('bsnw,nwd->bsnd', branches, lp['w_br'])
    g = jax.nn.sigmoid(gate.reshape(Bz, S, N_BRANCH, D_MODEL))
    out = jnp.einsum('bsd,de->bse', jnp.sum(g * up, axis=2), lp['w_out'])
    new_ctx = None if latent else dict(a_k=ka, a_v=va, b_k=kb, b_v=vb, ssm=h_fin, d_k=kd, d_v=vd)
    return out, new_ctx


def expert_choice_ffn(x, w_router, w1, w3, w2):
    Bz, T, D = x.shape
    cap = EC_FACTOR * T // N_EXPERTS
    aff = jax.nn.softmax(jnp.einsum('btd,de->bte', x, w_router).astype(F32), axis=-1)
    gval, idx = lax.top_k(jnp.swapaxes(aff, 1, 2), cap)
    xs = jax.vmap(lambda xb, ib: xb[ib])(x, idx)
    hid = jax.nn.silu(jnp.einsum('becd,edf->becf', xs, w1)) * jnp.einsum('becd,edf->becf', xs, w3)
    yo = jnp.einsum('becf,efd->becd', hid, w2) * gval[..., None].astype(x.dtype)
    return jax.vmap(lambda ib, yb: jnp.zeros((T, D), yb.dtype).at[ib.reshape(-1)].add(yb.reshape(-1, D)))(idx, yo)


def adaln(cvec, w, b):
    m = jnp.einsum('bd,dn->bn', jax.nn.silu(cvec), w) + b
    return m.reshape(m.shape[0], 6, D_MODEL)


def trunk_layer(x, mod, lp, li, ctx):
    shift1, scale1, gate1, shift2, scale2, gate2 = [mod[:, i, None, :] for i in range(6)]
    h = rmsnorm(x, lp['g_norm1']) * (1 + scale1) + shift1
    mix, new_ctx = token_mixers(h, lp, li, ctx)
    x = x + gate1 * mix
    h2 = rmsnorm(x, lp['g_norm2']) * (1 + scale2) + shift2
    x = x + gate2 * expert_choice_ffn(h2, lp['w_router'], lp['w_e1'], lp['w_e3'], lp['w_e2'])
    return x, new_ctx


def setup_inputs(seed: int = 0) -> dict:
    key = jax.random.key(seed)
    ks = iter(jax.random.split(key, 64))
    nrm = lambda shape, s: jax.random.normal(next(ks), shape, F32) * s
    gain = lambda shape: 1.0 + 0.02 * jax.random.normal(next(ks), shape, F32)
    inp = {}
    inp['x_prompt'] = nrm((BATCH, SEQ, D_MODEL), 1.0)
    inp['x_sample'] = nrm((DEC_BATCH, DEC_SEQ, D_MODEL), 1.0)
    inp['c'] = nrm((DEC_BATCH, D_MODEL), 1.0)
    inp['cache_a_k'] = nrm((DEC_BATCH, DEPTH, PAST_LEN, A_KV, HEAD_DIM), 1.0)
    inp['cache_a_v'] = nrm((DEC_BATCH, DEPTH, PAST_LEN, A_KV, HEAD_DIM), 1.0)
    inp['cache_b_k'] = nrm((DEC_BATCH, DEPTH, PAST_LEN, B_KV, HEAD_DIM), 1.0)
    inp['cache_b_v'] = nrm((DEC_BATCH, DEPTH, PAST_LEN, B_KV, HEAD_DIM), 1.0)
    inp['state_ssm'] = nrm((DEC_BATCH, DEPTH, 2, SSM_HEADS, SSM_HEAD_DIM, SSM_STATE), 0.1)
    inp['cache_d_k'] = nrm((DEC_BATCH, DEPTH, PAST_LEN, DIFF_HEADS, 2, DIFF_QK_DIM), 1.0)
    inp['cache_d_v'] = nrm((DEC_BATCH, DEPTH, PAST_LEN, DIFF_HEADS, DIFF_V_DIM), 1.0)
    inp['c_ctx'] = nrm((D_MODEL,), 1.0)
    inp['w_ada'] = nrm((DEPTH, D_MODEL, 6 * D_MODEL), 0.5 * D_MODEL ** -0.5)
    inp['b_ada'] = nrm((DEPTH, 6 * D_MODEL), 0.02)
    inp['g_norm1'] = gain((DEPTH, D_MODEL))
    inp['g_norm2'] = gain((DEPTH, D_MODEL))
    inp['w_in'] = nrm((DEPTH, D_MODEL, N_IN), D_MODEL ** -0.5)
    inp['g_qa'] = gain((DEPTH, HEAD_DIM))
    inp['g_ka'] = gain((DEPTH, HEAD_DIM))
    inp['g_qb'] = gain((DEPTH, HEAD_DIM))
    inp['g_kb'] = gain((DEPTH, HEAD_DIM))
    inp['sink_b'] = nrm((DEPTH, B_HEADS), 0.5)
    inp['conv_w'] = nrm((DEPTH, SSM_CONV, CONV_CH), SSM_CONV ** -0.5)
    inp['conv_b'] = nrm((DEPTH, CONV_CH), 0.02)
    dt0 = jnp.exp(jax.random.uniform(next(ks), (DEPTH, 2, SSM_HEADS), F32,
                                     minval=math.log(1e-3), maxval=math.log(1e-1)))
    inp['dt_bias'] = dt0 + jnp.log(-jnp.expm1(-dt0))
    inp['a_log'] = jnp.log(jax.random.uniform(next(ks), (DEPTH, 2, SSM_HEADS), F32, minval=1.0, maxval=16.0))
    inp['d_skip'] = 1.0 + 0.1 * jax.random.normal(next(ks), (DEPTH, SSM_HEADS), F32)
    inp['g_ssm'] = gain((DEPTH, SSM_INNER))
    inp['g_qd'] = gain((DEPTH, DIFF_QK_DIM))
    inp['g_kd'] = gain((DEPTH, DIFF_QK_DIM))
    inp['lam_q1'] = nrm((DEPTH, DIFF_QK_DIM), 0.1)
    inp['lam_k1'] = nrm((DEPTH, DIFF_QK_DIM), 0.1)
    inp['lam_q2'] = nrm((DEPTH, DIFF_QK_DIM), 0.1)
    inp['lam_k2'] = nrm((DEPTH, DIFF_QK_DIM), 0.1)
    inp['g_dout'] = gain((DEPTH, DIFF_V_DIM))
    inp['w_br'] = nrm((DEPTH, N_BRANCH, BRANCH_W, D_MODEL), BRANCH_W ** -0.5)
    inp['w_out'] = nrm((DEPTH, D_MODEL, D_MODEL), D_MODEL ** -0.5)
    inp['w_router'] = nrm((DEPTH, D_MODEL, N_EXPERTS), D_MODEL ** -0.5)
    inp['w_e1'] = nrm((DEPTH, N_EXPERTS, D_MODEL, EXPERT_FF), D_MODEL ** -0.5)
    inp['w_e3'] = nrm((DEPTH, N_EXPERTS, D_MODEL, EXPERT_FF), D_MODEL ** -0.5)
    inp['w_e2'] = nrm((DEPTH, N_EXPERTS, EXPERT_FF, D_MODEL), EXPERT_FF ** -0.5)
    return inp


def reference(x_prompt, x_sample, c, cache_a_k, cache_a_v, cache_b_k, cache_b_v, state_ssm,
              cache_d_k, cache_d_v, c_ctx, w_ada, b_ada, g_norm1, g_norm2, w_in, g_qa, g_ka,
              g_qb, g_kb, sink_b, conv_w, conv_b, dt_bias, a_log, d_skip, g_ssm, g_qd, g_kd,
              lam_q1, lam_k1, lam_q2, lam_k2, g_dout, w_br, w_out, w_router, w_e1, w_e3, w_e2):
    xp = x_prompt
    xl = x_sample
    new_a_k, new_a_v, new_b_k, new_b_v, new_ssm, new_d_k, new_d_v = [], [], [], [], [], [], []
    for li in range(DEPTH):
        lp = dict(g_norm1=g_norm1[li], g_norm2=g_norm2[li], w_in=w_in[li], g_qa=g_qa[li],
                  g_ka=g_ka[li], g_qb=g_qb[li], g_kb=g_kb[li], sink_b=sink_b[li],
                  conv_w=conv_w[li], conv_b=conv_b[li], dt_bias=dt_bias[li], a_log=a_log[li],
                  d_skip=d_skip[li], g_ssm=g_ssm[li], g_qd=g_qd[li], g_kd=g_kd[li],
                  lam_q1=lam_q1[li], lam_k1=lam_k1[li], lam_q2=lam_q2[li], lam_k2=lam_k2[li],
                  g_dout=g_dout[li], w_br=w_br[li], w_out=w_out[li], w_router=w_router[li],
                  w_e1=w_e1[li], w_e3=w_e3[li], w_e2=w_e2[li])
        mod_ctx = adaln(c_ctx[None, :], w_ada[li], b_ada[li])
        mod_lat = adaln(c, w_ada[li], b_ada[li])
        xp, ctx_new = trunk_layer(xp, mod_ctx, lp, li, None)
        new_a_k.append(ctx_new['a_k'])
        new_a_v.append(ctx_new['a_v'])
        new_b_k.append(ctx_new['b_k'])
        new_b_v.append(ctx_new['b_v'])
        new_ssm.append(ctx_new['ssm'])
        new_d_k.append(ctx_new['d_k'])
        new_d_v.append(ctx_new['d_v'])
        ctx_cached = dict(a_k=cache_a_k[:, li], a_v=cache_a_v[:, li], b_k=cache_b_k[:, li],
                          b_v=cache_b_v[:, li], ssm=state_ssm[:, li], d_k=cache_d_k[:, li],
                          d_v=cache_d_v[:, li])
        xl, _ = trunk_layer(xl, mod_lat, lp, li, ctx_cached)
    out_a_k = jnp.stack(new_a_k, axis=1)
    out_a_v = jnp.stack(new_a_v, axis=1)
    out_b_k = jnp.stack(new_b_k, axis=1)
    out_b_v = jnp.stack(new_b_v, axis=1)
    out_ssm = jnp.stack(new_ssm, axis=1)
    out_d_k = jnp.stack(new_d_k, axis=1)
    out_d_v = jnp.stack(new_d_v, axis=1)
    return (xp, xl, out_a_k, out_a_v, out_b_k, out_b_v, out_ssm, out_d_k, out_d_v)
```

```python
import functools
import math

import numpy as np
import jax
import jax.numpy as jnp
from jax import lax
from jax.experimental import pallas as pl
from jax.experimental.pallas import tpu as pltpu

F32 = jnp.float32
BF16 = jnp.bfloat16
I32 = jnp.int32

HEAD_DIM = 64
DIFF_QK_DIM = 32
GRID_W = 64
ROPE_THETA = 10000.0
NORM_EPS = 1e-6
WINDOW = 128
SSM_CHUNK = 128
SSM_HEADS = 4
SSM_HEAD_DIM = 64
SSM_STATE = 64
SSM_CONV = 5
EC_FACTOR = 2
BRANCH_W = 256
N_BRANCH = 4

LANES = 128
VMEM_LIMIT = 56 * 1024 * 1024

COL_A = (0, 512)
COL_B = (512, 1024)
COL_C = (1024, 1920)
COL_D = (1920, 2688)
W_MIX = 2688
N_MIX_SRC = 2568
DT_SRC = 1800


def _dot(a, b):
    return jnp.dot(a, b, preferred_element_type=F32)


def _dot_nt(a, b):
    return lax.dot_general(a, b, (((1,), (1,)), ((), ())), preferred_element_type=F32)


def _dot_tn(a, b):
    return lax.dot_general(a, b, (((0,), (0,)), ((), ())), preferred_element_type=F32)


def _split2(x):
    hi = x.astype(BF16)
    lo = (x - hi.astype(F32)).astype(BF16)
    return hi, lo


def _split3(x):
    hi = x.astype(BF16)
    r = x - hi.astype(F32)
    mid = r.astype(BF16)
    lo = (r - mid.astype(F32)).astype(BF16)
    return hi, mid, lo


def _silu(x):
    return x * jax.nn.sigmoid(x)


def _rms(x, g):
    ms = jnp.mean(x * x, axis=-1, keepdims=True)
    return x * lax.rsqrt(ms + NORM_EPS) * g


def _params(sem):
    return pltpu.CompilerParams(dimension_semantics=sem, vmem_limit_bytes=VMEM_LIMIT)


def _adaln_kernel(c_ref, w_ref, b_ref, o_ref):
    s_hi, s_lo = _split2(_silu(c_ref[...]))
    w_hi, w_lo = _split2(w_ref[0])
    o_ref[0] = _dot(s_hi, w_hi) + _dot(s_hi, w_lo) + _dot(s_lo, w_hi) + b_ref[0]


def _adaln(cvec, w_ada, b_ada):
    depth, d, n = w_ada.shape
    m = cvec.shape[0]
    tn = 1024
    return pl.pallas_call(
        _adaln_kernel,
        grid=(depth, n // tn),
        in_specs=[pl.BlockSpec((m, d), lambda l, j: (0, 0)),
                  pl.BlockSpec((1, d, tn), lambda l, j: (l, 0, j)),
                  pl.BlockSpec((1, 1, tn), lambda l, j: (l, 0, j))],
        out_specs=pl.BlockSpec((1, m, tn), lambda l, j: (l, 0, j)),
        out_shape=jax.ShapeDtypeStruct((depth, m, n), F32),
        compiler_params=_params(("arbitrary", "arbitrary")),
    )(cvec, w_ada, b_ada.reshape(depth, 1, n))


def _qknorm_rope_store(p, bd, gain, cos_ref, sin_ref, segs, gsize, n, o_ref):
    parts = []
    done = 0
    for (lo, hi) in segs:
        ps = p[:, lo:hi]
        ssum = _dot((ps * ps).astype(BF16), bd[0:hi - lo, 0:hi - lo])
        parts.append(ps * lax.rsqrt(ssum * (1.0 / gsize) + NORM_EPS) * gain[:, lo:hi])
        done = hi
    parts.append(p[:, done:])
    y = jnp.concatenate(parts, axis=1)
    for c in range(p.shape[1] // LANES):
        sl = slice(c * LANES, (c + 1) * LANES)
        ys = y[:, sl]
        up = pltpu.roll(ys, LANES - n, 1)
        dn = pltpu.roll(ys, n, 1)
        l128 = lax.broadcasted_iota(I32, ys.shape, 1)
        sw = jnp.where((l128 & (2 * n - 1)) < n, up, dn)
        o_ref[:, sl] = ys * cos_ref[:, sl] + sw * sin_ref[:, sl]


def _inproj_kernel(x_ref, mod_ref, g1_ref, w_ref, bda_ref, bdd_ref, ga_ref, gb_ref, gd_ref,
                   cosa_ref, sina_ref, cosd_ref, sind_ref, pa_ref, pb_ref, pc_ref, pd_ref):
    m = mod_ref[0]
    h = _rms(x_ref[...], g1_ref[...]) * (1.0 + m[1:2]) + m[0:1]
    hb = h.astype(BF16)
    pa = _dot(hb, w_ref[:, COL_A[0]:COL_A[1]])
    _qknorm_rope_store(pa, bda_ref[...], ga_ref[...], cosa_ref, sina_ref, ((0, 256), (256, 384)), HEAD_DIM, HEAD_DIM // 4, pa_ref)
    pb = _dot(hb, w_ref[:, COL_B[0]:COL_B[1]])
    _qknorm_rope_store(pb, bda_ref[...], gb_ref[...], cosa_ref, sina_ref, ((0, 256), (256, 384)), HEAD_DIM, HEAD_DIM // 4, pb_ref)
    pc_ref[...] = _dot(hb, w_ref[:, COL_C[0]:COL_C[1]])
    pd = _dot(hb, w_ref[:, COL_D[0]:COL_D[1]])
    _qknorm_rope_store(pd, bdd_ref[...], gd_ref[...], cosd_ref, sind_ref, ((0, 256), (256, 512)), DIFF_QK_DIM, DIFF_QK_DIM // 4, pd_ref)


def _inproj(x, mod, g1, w_mix, consts, geo, tm=256):
    t_all, d = x.shape
    nct = geo["t_ctx"] // tm
    tps = geo["r"] // tm

    def mod_row(i):
        return jnp.where(i < nct, 0, 1 + (i - nct) // tps)

    def rope_row(i):
        return jnp.where(i < nct, tps, (i - nct) % tps)

    full = lambda a: pl.BlockSpec(a.shape, lambda i: (0,) * a.ndim, pipeline_mode=pl.Buffered(1))
    widths = [COL_A[1] - COL_A[0], COL_B[1] - COL_B[0], COL_C[1] - COL_C[0], COL_D[1] - COL_D[0]]
    tab = lambda w: pl.BlockSpec((tm, w), lambda i: (rope_row(i), 0))
    return pl.pallas_call(
        _inproj_kernel,
        grid=(t_all // tm,),
        in_specs=[pl.BlockSpec((tm, d), lambda i: (i, 0)),
                  pl.BlockSpec((1, 6, d), lambda i: (mod_row(i), 0, 0)),
                  full(g1), full(w_mix), full(consts["bda"]), full(consts["bdd"]),
                  full(consts["ga"]), full(consts["gb"]), full(consts["gd"]),
                  tab(512), tab(512), tab(768), tab(768)],
        out_specs=[pl.BlockSpec((tm, w), lambda i: (i, 0)) for w in widths],
        out_shape=[jax.ShapeDtypeStruct((t_all, w), F32) for w in widths],
        compiler_params=_params(("arbitrary",)),
    )(x, mod, g1, w_mix, consts["bda"], consts["bdd"], consts["ga"], consts["gb"], consts["gd"],
      consts["cosa"], consts["sina"], consts["cosd"], consts["sind"])


def _expand_kv_heads(v):
    lane = lax.broadcasted_iota(I32, v.shape, 1)
    r = pltpu.roll(v, HEAD_DIM, 1)
    return jnp.concatenate([jnp.where(lane < HEAD_DIM, v, r), jnp.where(lane >= HEAD_DIM, v, r)], axis=1)


def _attn_kernel(kind, geo, li, p_ref, ck_ref, cv_ref, sink_ref, lam_ref, gdo_ref, y_ref, kb_ref, vb_ref):
    r, past, tq, ng = geo["r"], geo["past"], geo["tq"], geo["n_ctx_groups"]
    s = pl.program_id(0)
    qi = pl.program_id(1)
    is_ctx = s < ng
    diff = kind == "D"
    kcols = (256, 512) if diff else (256, 384)
    vcols = (512, 768) if diff else (384, 512)
    wk = kcols[1] - kcols[0]
    dqk = DIFF_QK_DIM if diff else HEAD_DIM
    scale = 1.0 / math.sqrt(dqk)

    @pl.when(qi == 0)
    def _prep():
        kb_ref[0:r, :] = p_ref[:, kcols[0]:kcols[1]].astype(BF16)
        v = p_ref[:, vcols[0]:vcols[1]]
        vb_ref[0:r, :] = (v if diff else _expand_kv_heads(v)).astype(BF16)

        @pl.when(jnp.logical_not(is_ctx))
        def _cache():
            kb_ref[r:r + past, :] = ck_ref[0, 0].astype(BF16)
            cv = cv_ref[0, 0]
            vb_ref[r:r + past, :] = (cv if diff else _expand_kv_heads(cv)).astype(BF16)

    q0 = pl.multiple_of(qi * tq, tq)
    q = p_ref[pl.ds(q0, tq), 0:256] * scale
    lane256 = lax.broadcasted_iota(I32, (tq, 256), 1)
    lane128 = lax.broadcasted_iota(I32, (tq, LANES), 1)

    def q_for(u):
        if diff:
            slab = q[:, (u // 4) * LANES:(u // 4 + 1) * LANES]
            lo = (u % 4) * DIFF_QK_DIM
            keep = (lane128 >= lo) & (lane128 < lo + DIFF_QK_DIM)
            return jnp.where(keep, slab, 0.0).astype(BF16), u // 4
        slab = q[:, (u // 2) * LANES:(u // 2 + 1) * LANES]
        g = u // 2
        if (u % 2) != g:
            slab = pltpu.roll(slab, HEAD_DIM, 1)
        keep = (lane128 < HEAD_DIM) if g == 0 else (lane128 >= HEAD_DIM)
        return jnp.where(keep, slab, 0.0).astype(BF16), 0

    def softmax_pv(u, srcs, sink):
        qm, kslab = q_for(u)
        ss = []
        for (st, n, mask) in srcs:
            kk = kb_ref[pl.ds(st, n), kslab * LANES:(kslab + 1) * LANES]
            sc = _dot_nt(qm, kk)
            if mask is not None:
                sc = jnp.where(mask, sc, -jnp.inf)
            ss.append(sc)
        mx = ss[0].max(axis=-1, keepdims=True)
        for sc in ss[1:]:
            mx = jnp.maximum(mx, sc.max(axis=-1, keepdims=True))
        if sink is not None:
            mx = jnp.maximum(mx, sink)
        l = jnp.zeros((tq, 1), F32)
        o = jnp.zeros((tq, 256), F32)
        for (st, n, _), sc in zip(srcs, ss):
            pexp = jnp.exp(sc - mx)
            l = l + pexp.sum(axis=-1, keepdims=True)
            o = o + _dot(pexp.astype(BF16), vb_ref[pl.ds(st, n), :])
        if sink is not None:
            l = l + jnp.exp(sink - mx)
        return o, l

    if diff:
        lam_init = 0.8 - 0.6 * math.exp(-0.3 * li)
        lv = lam_ref[...]
        lam = (jnp.exp(jnp.sum(lv[0:1] * lv[1:2], axis=1, keepdims=True))
               - jnp.exp(jnp.sum(lv[2:3] * lv[3:4], axis=1, keepdims=True)) + lam_init)

    def run(srcs):
        acc = jnp.zeros((tq, 256), F32)
        for h in range(4):
            head = (lane256 >= h * HEAD_DIM) & (lane256 < (h + 1) * HEAD_DIM)
            if diff:
                o1, l1 = softmax_pv(2 * h, srcs, None)
                o2, l2 = softmax_pv(2 * h + 1, srcs, None)
                res = o1 / l1 - lam * (o2 / l2)
            else:
                sink = sink_ref[li, h] if kind == "B" else None
                o1, l1 = softmax_pv(h, srcs, sink)
                res = o1 / l1
            acc = jnp.where(head, res, acc)
        if diff:
            sq = acc * acc
            inv = jnp.zeros((tq, 256), F32)
            for h in range(4):
                head = (lane256 >= h * HEAD_DIM) & (lane256 < (h + 1) * HEAD_DIM)
                ms = jnp.sum(jnp.where(head, sq, 0.0), axis=-1, keepdims=True) * (1.0 / HEAD_DIM)
                inv = jnp.where(head, lax.rsqrt(ms + NORM_EPS), inv)
            acc = acc * inv * gdo_ref[...] * (1.0 - lam_init)
        y_ref[...] = acc

    @pl.when(is_ctx)
    def _ctx():
        run([(q0, tq, None)])

    @pl.when(jnp.logical_not(is_ctx))
    def _lat():
        if kind == "B":
            span = tq + 2 * WINDOW
            st = pl.multiple_of(jnp.clip(q0 - WINDOW, 0, r - span), WINDOW)
            qpos = q0 + lax.broadcasted_iota(I32, (tq, span), 0)
            kpos = st + lax.broadcasted_iota(I32, (tq, span), 1)
            ok = jnp.abs(qpos - kpos) <= WINDOW
            run([(st, span, ok), (r, past, None)])
        else:
            run([(0, r + past, None)])


def _attention(kind, geo, li, p, ck, cv, sink, lam, gdo):
    t_all, w = p.shape
    r, past, tq, ng = geo["r"], geo["past"], geo["tq"], geo["n_ctx_groups"]
    ns = t_all // r
    wk = 256 if kind == "D" else 128
    cache_idx = lambda s, q: (jnp.maximum(s - ng, 0), li, 0, 0)
    return pl.pallas_call(
        functools.partial(_attn_kernel, kind, geo, li),
        grid=(ns, r // tq),
        in_specs=[pl.BlockSpec((r, w), lambda s, q: (s, 0)),
                  pl.BlockSpec((1, 1, past, ck.shape[-1]), cache_idx),
                  pl.BlockSpec((1, 1, past, cv.shape[-1]), cache_idx),
                  pl.BlockSpec(memory_space=pltpu.SMEM),
                  pl.BlockSpec(lam.shape, lambda s, q: (0, 0)),
                  pl.BlockSpec(gdo.shape, lambda s, q: (0, 0))],
        out_specs=pl.BlockSpec((tq, 256), lambda s, q: (s * (r // tq) + q, 0)),
        out_shape=jax.ShapeDtypeStruct((t_all, 256), F32),
        scratch_shapes=[pltpu.VMEM((r + past, wk), BF16), pltpu.VMEM((r + past, 256), BF16)],
        compiler_params=_params(("arbitrary", "arbitrary")),
    )(p, ck, cv, sink, lam, gdo)


def _expand_heads(v, d, rows):
    lane = lax.broadcasted_iota(I32, (rows, SSM_HEADS * SSM_HEAD_DIM), 1)
    out = jnp.zeros((rows, SSM_HEADS * SSM_HEAD_DIM), F32)
    for h in range(SSM_HEADS):
        c = d * SSM_HEADS + h
        out = jnp.where((lane >= h * SSM_HEAD_DIM) & (lane < (h + 1) * SSM_HEAD_DIM), v[:, c:c + 1], out)
    return out


def _ssd_kernel(geo, li, p_ref, st0_ref, cw_ref, cb_ref, dtb_ref, alog_ref, dsk_ref, gs_ref,
                y_ref, so_ref, xpad, u_scr, dt_scr, la_scr, y_scr, s2_ref):
    r, seq, ng = geo["r"], geo["seq"], geo["n_ctx_groups"]
    q = SSM_CHUNK
    s = pl.program_id(0)
    is_ctx = s < ng
    seq_len = jnp.where(is_ctx, seq, r)
    inner = SSM_HEADS * SSM_HEAD_DIM
    conv_ch = inner + 2 * LANES

    xpad[0:8, :] = jnp.zeros((8, conv_ch), F32)
    xpad[r + 8:r + 16, :] = jnp.zeros((8, conv_ch), F32)
    xpad[8:r + 8, :] = p_ref[:, inner:inner + conv_ch]
    a_neg = -jnp.exp(alog_ref[...])

    def pre(c, carry):
        r0 = pl.multiple_of(c * q, q)
        big = xpad[pl.ds(r0, q + 16), :]
        tpos = (r0 + lax.broadcasted_iota(I32, (q, 1), 0)) & (seq_len - 1)
        acc = jnp.zeros((q, conv_ch), F32) + cb_ref[...]
        for j in range(SSM_CONV):
            sh = pltpu.roll(big, (q + 16) - (6 + j), 0)[0:q]
            tj = tpos + (j - SSM_CONV // 2)
            ok = (tj >= 0) & (tj < seq_len)
            acc = acc + jnp.where(ok, sh, 0.0) * cw_ref[j:j + 1, :]
        u_scr[pl.ds(r0, q), :] = _silu(acc)
        raw = p_ref[pl.ds(r0, q), inner + conv_ch:inner + conv_ch + LANES] + dtb_ref[...]
        dt = jnp.maximum(raw, 0.0) + jnp.log1p(jnp.exp(-jnp.abs(raw)))
        dt_scr[pl.ds(r0, q), :] = dt
        la_scr[pl.ds(r0, q), :] = dt * a_neg
        return carry

    lax.fori_loop(0, r // q, pre, 0)

    row_i = lax.broadcasted_iota(I32, (q, q), 0)
    col_i = lax.broadcasted_iota(I32, (q, q), 1)
    lane128 = lax.broadcasted_iota(I32, (q, LANES), 1)
    lane256 = lax.broadcasted_iota(I32, (q, inner), 1)
    row256 = lax.broadcasted_iota(I32, (inner, 1), 0)
    gmask = (lax.broadcasted_iota(I32, (inner, LANES), 0) // (2 * SSM_HEAD_DIM)
             == lax.broadcasted_iota(I32, (inner, LANES), 1) // SSM_STATE)

    def chunk(d, r0, first):
        keep = (col_i <= row_i) if d == 0 else (col_i >= row_i)
        tri = jnp.where(keep, 1.0, 0.0).astype(BF16)
        a1, a2, a3 = _split3(la_scr[pl.ds(r0, q), :])
        cum = _dot(tri, a1) + _dot(tri, a2) + _dot(tri, a3)
        cum_t = cum.T
        edge = q - 1 if d == 0 else 0
        tot_lane = cum[edge:edge + 1, :]
        tot_row = cum_t[:, edge:edge + 1]
        u = u_scr[pl.ds(r0, q), :]
        xs, bm, cm = u[:, 0:inner], u[:, inner:inner + LANES], u[:, inner + LANES:inner + 2 * LANES]
        xdt = xs * _expand_heads(dt_scr[pl.ds(r0, q), :], d, q)
        xdt_b = xdt.astype(BF16)
        bm_b = bm.astype(BF16)
        cb = [_dot_nt(jnp.where((lane128 // SSM_STATE) == g, cm, 0.0).astype(BF16), bm_b) for g in range(2)]
        y = jnp.zeros((q, inner), F32)
        for h in range(SSM_HEADS):
            c = d * SSM_HEADS + h
            seg = cum[:, c:c + 1] - cum_t[c:c + 1, :]
            lm = jnp.exp(jnp.where(keep, seg, -jnp.inf))
            res = _dot((cb[h // 2] * lm).astype(BF16), xdt_b)
            y = jnp.where((lane256 >= h * SSM_HEAD_DIM) & (lane256 < (h + 1) * SSM_HEAD_DIM), res, y)
        s2 = s2_ref[...]
        y = y + _dot_nt(cm.astype(BF16), s2.astype(BF16)) * _expand_heads(jnp.exp(cum), d, q)
        if first:
            y_scr[pl.ds(r0, q), :] = y
        else:
            y_scr[pl.ds(r0, q), :] += y
        xw = (xdt * _expand_heads(jnp.exp(tot_lane - cum), d, q)).astype(BF16)
        upd = _dot_tn(xw, bm_b)
        cd = jnp.zeros((inner, 1), F32)
        for h in range(SSM_HEADS):
            c = d * SSM_HEADS + h
            cd = jnp.where((row256 >= h * SSM_HEAD_DIM) & (row256 < (h + 1) * SSM_HEAD_DIM),
                           jnp.exp(tot_row[c:c + 1, :]), cd)
        s2_ref[...] = s2 * cd + jnp.where(gmask, upd, 0.0)

    def finish(r0, n):
        def fin(c, carry):
            rr = pl.multiple_of(r0 + c * q, q)
            y = y_scr[pl.ds(rr, q), :] + u_scr[pl.ds(rr, q), 0:inner] * dsk_ref[...]
            y = y * _silu(p_ref[pl.ds(rr, q), 0:inner])
            y_ref[pl.ds(rr, q), :] = _rms(y, gs_ref[...])
            return carry
        lax.fori_loop(0, n, fin, 0)

    @pl.when(is_ctx)
    def _ctx():
        nch = seq // q

        def one_seq(j, carry):
            r0 = pl.multiple_of(j * seq, seq)
            for d in range(2):
                s2_ref[...] = jnp.zeros((inner, LANES), F32)
                for ci in range(nch):
                    c = ci if d == 0 else nch - 1 - ci
                    chunk(d, r0 + c * q, d == 0)
                s2 = s2_ref[...]
                so_ref[j, d] = (s2 + pltpu.roll(s2, SSM_STATE, 1))[:, 0:SSM_STATE]
            return carry

        lax.fori_loop(0, r // seq, one_seq, 0)
        finish(0, r // q)

    @pl.when(jnp.logical_not(is_ctx))
    def _lat():
        nch = r // q
        for d in range(2):
            s2_ref[...] = st0_ref[0, 0, d]

            def body(ci, carry):
                c = ci if d == 0 else nch - 1 - ci
                chunk(d, pl.multiple_of(c * q, q), d == 0)
                return carry

            lax.fori_loop(0, nch, body, 0)
        finish(0, r // q)


def _ssd(geo, li, pc, st0, cw, cb, dtb, alog, dsk, gs):
    t_all, w = pc.shape
    r, seq, ng = geo["r"], geo["seq"], geo["n_ctx_groups"]
    ns = t_all // r
    spg = r // seq
    inner = SSM_HEADS * SSM_HEAD_DIM
    full = lambda a: pl.BlockSpec(a.shape, lambda s: (0,) * a.ndim)
    return pl.pallas_call(
        functools.partial(_ssd_kernel, geo, li),
        grid=(ns,),
        in_specs=[pl.BlockSpec((r, w), lambda s: (s, 0)),
                  pl.BlockSpec((1, 1, 2, inner, LANES), lambda s: (jnp.maximum(s - ng, 0), li, 0, 0, 0)),
                  full(cw), full(cb), full(dtb), full(alog), full(dsk), full(gs)],
        out_specs=[pl.BlockSpec((r, inner), lambda s: (s, 0)),
                   pl.BlockSpec((spg, 2, inner, SSM_STATE), lambda s: (jnp.minimum(s, ng - 1), 0, 0, 0))],
        out_shape=[jax.ShapeDtypeStruct((t_all, inner), F32),
                   jax.ShapeDtypeStruct((ng * spg, 2, inner, SSM_STATE), F32)],
        scratch_shapes=[pltpu.VMEM((r + 16, 2 * inner), F32), pltpu.VMEM((r, 2 * inner), F32),
                        pltpu.VMEM((r, LANES), F32), pltpu.VMEM((r, LANES), F32),
                        pltpu.VMEM((r, inner), F32), pltpu.VMEM((inner, LANES), F32)],
        compiler_params=_params(("arbitrary",)),
    )(pc, st0, cw, cb, dtb, alog, dsk, gs)


def _merge_kernel(x_ref, mod_ref, g1_ref, g2_ref, ya_ref, yb_ref, yc_ref, yd_ref, wg_ref, wbr_ref, wo_ref,
                  wr_ref, x1_ref, h2_ref, aff_ref):
    d = x_ref.shape[1]
    x = x_ref[...]
    m = mod_ref[0]
    hb = (_rms(x, g1_ref[...]) * (1.0 + m[1:2]) + m[0:1]).astype(BF16)
    merged = jnp.zeros(x.shape, F32)
    for n, y_ref in enumerate((ya_ref, yb_ref, yc_ref, yd_ref)):
        gate = jax.nn.sigmoid(_dot(hb, wg_ref[:, n * d:(n + 1) * d]))
        merged = merged + gate * _dot(y_ref[...].astype(BF16), wbr_ref[n])
    x1 = x + m[2:3] * _dot(merged.astype(BF16), wo_ref[...])
    x1_ref[...] = x1
    h2 = _rms(x1, g2_ref[...]) * (1.0 + m[4:5]) + m[3:4]
    h2_ref[...] = h2.astype(BF16)
    h_hi, h_lo = _split2(h2)
    w_hi, w_lo = _split2(wr_ref[...])
    logit = _dot_nt(w_hi, h_hi) + _dot_nt(w_hi, h_lo) + _dot_nt(w_lo, h_hi)
    e = jnp.exp(logit - logit.max(axis=0, keepdims=True))
    aff_ref[...] = e / e.sum(axis=0, keepdims=True)


def _merge(x, mod, g1, g2, ys, wg, wbr, wo, wr_t, geo, tm=256):
    t_all, d = x.shape
    ne = wr_t.shape[0]
    nct = geo["t_ctx"] // tm
    tps = geo["r"] // tm
    mod_row = lambda i: jnp.where(i < nct, 0, 1 + (i - nct) // tps)
    full = lambda a: pl.BlockSpec(a.shape, lambda i: (0,) * a.ndim, pipeline_mode=pl.Buffered(1))
    ytile = pl.BlockSpec((tm, BRANCH_W), lambda i: (i, 0))
    return pl.pallas_call(
        _merge_kernel,
        grid=(t_all // tm,),
        in_specs=[pl.BlockSpec((tm, d), lambda i: (i, 0)),
                  pl.BlockSpec((1, 6, d), lambda i: (mod_row(i), 0, 0)),
                  full(g1), full(g2), ytile, ytile, ytile, ytile, full(wg), full(wbr), full(wo), full(wr_t)],
        out_specs=[pl.BlockSpec((tm, d), lambda i: (i, 0)),
                   pl.BlockSpec((tm, d), lambda i: (i, 0)),
                   pl.BlockSpec((ne, tm), lambda i: (0, i))],
        out_shape=[jax.ShapeDtypeStruct((t_all, d), F32),
                   jax.ShapeDtypeStruct((t_all, d), BF16),
                   jax.ShapeDtypeStruct((ne, t_all), F32)],
        compiler_params=_params(("arbitrary",)),
    )(x, mod, g1, g2, *ys, wg, wbr, wo, wr_t)


def _kth_largest_bits(a, cap):
    def body(i, thr):
        cand = thr | lax.shift_left(jnp.int32(1), 30 - i)
        cnt = jnp.sum(jnp.where(a >= lax.bitcast_convert_type(cand, F32), 1.0, 0.0), axis=1, keepdims=True)
        return jnp.where(cnt >= cap, cand, thr)
    return lax.fori_loop(0, 31, body, jnp.zeros((a.shape[0], 1), I32))


def _route_kernel(geo, aff_ref, slot_ref):
    r, seq, ng = geo["r"], geo["seq"], geo["n_ctx_groups"]
    ne = aff_ref.shape[0]
    blk = 256
    s = pl.program_id(0)
    upper = jnp.where(lax.broadcasted_iota(I32, (blk, blk), 0) < lax.broadcasted_iota(I32, (blk, blk), 1),
                      1.0, 0.0).astype(BF16)

    def select(a, cap, nblk):
        thr = _kth_largest_bits(a, cap)
        gt = a >= lax.bitcast_convert_type(thr + 1, F32)
        eq = (a >= lax.bitcast_convert_type(thr, F32)) & jnp.logical_not(gt)
        need = cap - jnp.sum(jnp.where(gt, 1.0, 0.0), axis=1, keepdims=True)
        eq_f = jnp.where(eq, 1.0, 0.0)
        outs = []
        carry_e = jnp.zeros((a.shape[0], 1), F32)
        carry_s = jnp.zeros((a.shape[0], 1), F32)
        for b in range(nblk):
            sl = slice(b * blk, (b + 1) * blk)
            pe = _dot(eq_f[:, sl].astype(BF16), upper) + carry_e
            sel = gt[:, sl] | (eq[:, sl] & (pe < need))
            sel_f = jnp.where(sel, 1.0, 0.0)
            ps = _dot(sel_f.astype(BF16), upper) + carry_s
            outs.append(jnp.where(sel, ps, -1.0))
            carry_e = carry_e + eq_f[:, sl].sum(axis=1, keepdims=True)
            carry_s = carry_s + sel_f.sum(axis=1, keepdims=True)
        return outs

    @pl.when(s < ng)
    def _ctx():
        spg = r // seq
        cap = EC_FACTOR * seq // ne
        a = jnp.concatenate([aff_ref[:, j * seq:(j + 1) * seq] for j in range(spg)], axis=0)
        out = select(a, cap, seq // blk)
        out = out[0] if len(out) == 1 else jnp.concatenate(out, axis=1)
        for j in range(spg):
            o = out[j * ne:(j + 1) * ne]
            slot_ref[:, j * seq:(j + 1) * seq] = jnp.where(o >= 0, o + j * cap, -1.0).astype(I32)

    @pl.when(s >= ng)
    def _lat():
        cap = EC_FACTOR * r // ne
        out = select(aff_ref[...], cap, r // blk)
        for b, o in enumerate(out):
            slot_ref[:, b * blk:(b + 1) * blk] = o.astype(I32)


def _route(aff_t, geo):
    ne, t_all = aff_t.shape
    r = geo["r"]
    return pl.pallas_call(
        functools.partial(_route_kernel, geo),
        grid=(t_all // r,),
        in_specs=[pl.BlockSpec((ne, r), lambda s: (0, s))],
        out_specs=pl.BlockSpec((ne, r), lambda s: (0, s)),
        out_shape=jax.ShapeDtypeStruct((ne, t_all), I32),
        compiler_params=_params(("arbitrary",)),
    )(aff_t)


def _expert_kernel(x_ref, slot_ref, aff_ref, w1_ref, w3_ref, w2_ref, yo_ref, wb1, wb3, wb2):
    @pl.when(pl.program_id(1) == 0)
    def _cast():
        wb1[...] = w1_ref[0].astype(BF16)
        wb3[...] = w3_ref[0].astype(BF16)
        wb2[...] = w2_ref[0].astype(BF16)

    cap = yo_ref.shape[2]
    slot = slot_ref[0]
    pick = lax.broadcasted_iota(I32, (cap, slot.shape[1]), 0) == slot
    xs = _dot(jnp.where(pick, 1.0, 0.0).astype(BF16), x_ref[...]).astype(BF16)
    gs = jnp.sum(jnp.where(pick, aff_ref[0], 0.0), axis=1, keepdims=True)
    hid = _silu(_dot(xs, wb1[...])) * _dot(xs, wb3[...])
    yo_ref[0, 0] = (_dot(hid.astype(BF16), wb2[...]) * gs).astype(BF16)


def _experts(h2, slot3, aff3, w1, w3, w2, geo):
    t_all, d = h2.shape
    ne, _, f = w1.shape
    r = geo["r"]
    ns = t_all // r
    cap = EC_FACTOR * r // ne
    wspec = lambda a: pl.BlockSpec((1,) + a.shape[1:], lambda e, s: (e, 0, 0))
    return pl.pallas_call(
        _expert_kernel,
        grid=(ne, ns),
        in_specs=[pl.BlockSpec((r, d), lambda e, s: (s, 0)),
                  pl.BlockSpec((1, 1, r), lambda e, s: (e, 0, s)),
                  pl.BlockSpec((1, 1, r), lambda e, s: (e, 0, s)),
                  wspec(w1), wspec(w3), wspec(w2)],
        out_specs=pl.BlockSpec((1, 1, cap, d), lambda e, s: (s, e, 0, 0)),
        out_shape=jax.ShapeDtypeStruct((ns, ne, cap, d), BF16),
        scratch_shapes=[pltpu.VMEM((d, f), BF16), pltpu.VMEM((d, f), BF16), pltpu.VMEM((f, d), BF16)],
        compiler_params=_params(("arbitrary", "arbitrary")),
    )(h2, slot3, aff3, w1, w3, w2)


def _scatter_kernel(ng, split, x_ref, mod_ref, slot_ref, yo_ref, *o_refs):
    tt = x_ref.shape[0]
    _, ne, cap, d = yo_ref.shape
    slot = slot_ref[...]
    lane = lax.broadcasted_iota(I32, (tt, cap), 1)
    onehot = jnp.concatenate([jnp.where(slot[:, e:e + 1] == lane, 1.0, 0.0).astype(BF16) for e in range(ne)],
                             axis=1)
    ffn = _dot(onehot, yo_ref[0].reshape(ne * cap, d))
    out = x_ref[...] + mod_ref[0][5:6] * ffn
    if not split:
        o_refs[0][...] = out
    else:
        is_ctx = pl.program_id(0) < ng

        @pl.when(is_ctx)
        def _():
            o_refs[0][...] = out

        @pl.when(jnp.logical_not(is_ctx))
        def _():
            o_refs[1][...] = out


def _scatter(x1, mod, slot_t, yo, geo, split=False, tt=512):
    t_all, d = x1.shape
    ns, ne, cap, _ = yo.shape
    r = geo["r"]
    ng = geo["n_ctx_groups"]
    tpg = r // tt
    nct = ng * tpg
    mod_row = lambda s, i: jnp.where(s < ng, 0, 1 + s - ng)
    if split:
        out_specs = [pl.BlockSpec((tt, d), lambda s, i: (jnp.minimum(s * tpg + i, nct - 1), 0)),
                     pl.BlockSpec((tt, d), lambda s, i: (jnp.maximum(s * tpg + i - nct, 0), 0))]
        out_shape = [jax.ShapeDtypeStruct((nct * tt, d), F32), jax.ShapeDtypeStruct((t_all - nct * tt, d), F32)]
    else:
        out_specs = pl.BlockSpec((tt, d), lambda s, i: (s * tpg + i, 0))
        out_shape = jax.ShapeDtypeStruct((t_all, d), F32)
    return pl.pallas_call(
        functools.partial(_scatter_kernel, ng, split),
        grid=(ns, tpg),
        in_specs=[pl.BlockSpec((tt, d), lambda s, i: (s * tpg + i, 0)),
                  pl.BlockSpec((1, 6, d), lambda s, i: (mod_row(s, i), 0, 0)),
                  pl.BlockSpec((tt, ne), lambda s, i: (s * tpg + i, 0)),
                  pl.BlockSpec((1, ne, cap, d), lambda s, i: (s, 0, 0, 0))],
        out_specs=out_specs,
        out_shape=out_shape,
        compiler_params=_params(("arbitrary", "arbitrary")),
    )(x1, mod, slot_t, yo)


def _rope_table(s, d, n_qk_lanes, n_v_lanes, pad_rows):
    n = d // 4
    t = np.arange(s)
    rows = (t // GRID_W).astype(np.float64)
    cols = (t % GRID_W).astype(np.float64)
    inv = ROPE_THETA ** (-np.arange(n, dtype=np.float64) / n)
    ang = np.stack([rows[:, None] * inv, cols[:, None] * inv], axis=1)
    cos, sin = np.cos(ang), np.sin(ang)
    cos_h = np.concatenate([cos[:, 0], cos[:, 0], cos[:, 1], cos[:, 1]], axis=-1)
    sin_h = np.concatenate([-sin[:, 0], sin[:, 0], -sin[:, 1], sin[:, 1]], axis=-1)
    reps = n_qk_lanes // d
    cos_f = np.concatenate([np.tile(cos_h, (1, reps)), np.ones((s, n_v_lanes))], axis=1)
    sin_f = np.concatenate([np.tile(sin_h, (1, reps)), np.zeros((s, n_v_lanes))], axis=1)
    w = n_qk_lanes + n_v_lanes
    cos_f = np.concatenate([cos_f, np.ones((pad_rows, w))], axis=0)
    sin_f = np.concatenate([sin_f, np.zeros((pad_rows, w))], axis=0)
    return jnp.asarray(cos_f.astype(np.float32)), jnp.asarray(sin_f.astype(np.float32))


def _block_diag_ones(w, gsize):
    i = np.arange(w)
    return jnp.asarray((i[:, None] // gsize == i[None, :] // gsize).astype(np.float32), dtype=BF16)


def _pad_lanes(a, w):
    return jnp.pad(a, ((0, 0), (0, w - a.shape[1])))


def kernel(x_prompt, x_sample, c, cache_a_k, cache_a_v, cache_b_k, cache_b_v, state_ssm, cache_d_k, cache_d_v, c_ctx, w_ada, b_ada, g_norm1, g_norm2, w_in, g_qa, g_ka, g_qb, g_kb, sink_b, conv_w, conv_b, dt_bias, a_log, d_skip, g_ssm, g_qd, g_kd, lam_q1, lam_k1, lam_q2, lam_k2, g_dout, w_br, w_out, w_router, w_e1, w_e3, w_e2):
    batch, seq, d = x_prompt.shape
    dec_batch, dec_seq, _ = x_sample.shape
    depth = w_in.shape[0]
    past = cache_a_k.shape[2]
    ne = w_router.shape[2]
    r = dec_seq
    t_ctx = batch * seq
    assert t_ctx % r == 0 and r % seq == 0 and seq == 256 and past % 16 == 0
    tm = 512
    assert t_ctx % tm == 0 and r % tm == 0
    geo = dict(r=r, seq=seq, past=past, tq=256, t_ctx=t_ctx, n_ctx_groups=t_ctx // r)
    ng = geo["n_ctx_groups"]

    cosa, sina = _rope_table(r, HEAD_DIM, 384, 128, tm)
    cosd, sind = _rope_table(r, DIFF_QK_DIM, 512, 256, tm)
    bda = _block_diag_ones(256, HEAD_DIM)
    bdd = _block_diag_ones(256, DIFF_QK_DIM)
    ones = lambda n: jnp.ones((n,), F32)
    inner = SSM_HEADS * SSM_HEAD_DIM
    gsel = (np.arange(inner)[:, None] // (2 * SSM_HEAD_DIM)) == (np.arange(LANES)[None, :] // SSM_STATE)

    nm = 16
    cvec = jnp.concatenate([c_ctx[None, :], c, jnp.zeros((nm - 1 - dec_batch, d), F32)], axis=0)
    mod_all = _adaln(cvec, w_ada, b_ada)

    x = jnp.concatenate([x_prompt.reshape(t_ctx, d), x_sample.reshape(dec_batch * r, d)], axis=0)
    ck_a = cache_a_k.reshape(dec_batch, depth, past, -1)
    cv_a = cache_a_v.reshape(dec_batch, depth, past, -1)
    ck_b = cache_b_k.reshape(dec_batch, depth, past, -1)
    cv_b = cache_b_v.reshape(dec_batch, depth, past, -1)
    ck_d = cache_d_k.reshape(dec_batch, depth, past, -1)
    cv_d = cache_d_v.reshape(dec_batch, depth, past, -1)
    st0 = state_ssm.reshape(dec_batch, depth, 2, inner, SSM_STATE)
    st0 = jnp.where(gsel, jnp.concatenate([st0, st0], axis=-1), 0.0)

    new = {k: [] for k in ("a_k", "a_v", "b_k", "b_v", "ssm", "d_k", "d_v")}
    for li in range(depth):
        mod = mod_all[li].reshape(nm, 6, d)
        wl = w_in[li]
        w_mix = jnp.concatenate([wl[:, :DT_SRC], jnp.zeros((d, COL_C[1] - COL_C[0] - 776), F32),
                                 wl[:, DT_SRC:N_MIX_SRC]], axis=1).astype(BF16)
        w_gate = wl[:, N_MIX_SRC:].astype(BF16)
        consts = dict(
            bda=bda, bdd=bdd, cosa=cosa, sina=sina, cosd=cosd, sind=sind,
            ga=jnp.concatenate([jnp.tile(g_qa[li], 4), jnp.tile(g_ka[li], 2), ones(128)])[None, :],
            gb=jnp.concatenate([jnp.tile(g_qb[li], 4), jnp.tile(g_kb[li], 2), ones(128)])[None, :],
            gd=jnp.concatenate([jnp.tile(g_qd[li], 8), jnp.tile(g_kd[li], 8), ones(256)])[None, :])
        g1 = g_norm1[li][None, :]
        g2 = g_norm2[li][None, :]
        pa, pb, pc, pd = _inproj(x, mod, g1, w_mix, consts, geo, tm)

        lam = jnp.stack([lam_q1[li], lam_k1[li], lam_q2[li], lam_k2[li]], axis=0)
        gdo = jnp.tile(g_dout[li], 4)[None, :]
        ya = _attention("A", geo, li, pa, ck_a, cv_a, sink_b, lam, gdo)
        yb = _attention("B", geo, li, pb, ck_b, cv_b, sink_b, lam, gdo)
        yd = _attention("D", geo, li, pd, ck_d, cv_d, sink_b, lam, gdo)
        yc, ssm_new = _ssd(geo, li, pc, st0,
                           jnp.pad(conv_w[li], ((0, 8 - SSM_CONV), (0, 0))), conv_b[li][None, :],
                           _pad_lanes(dt_bias[li].reshape(1, -1), LANES), _pad_lanes(a_log[li].reshape(1, -1), LANES),
                           jnp.repeat(d_skip[li], SSM_HEAD_DIM)[None, :], g_ssm[li][None, :])

        x1, h2, aff_t = _merge(x, mod, g1, g2, (ya, yb, yc, yd), w_gate, w_br[li].astype(BF16),
                               w_out[li].astype(BF16), w_router[li].T, geo, tm)
        slot = _route(aff_t, geo)
        yo = _experts(h2, slot.reshape(ne, 1, -1), aff_t.reshape(ne, 1, -1), w_e1[li], w_e3[li], w_e2[li], geo)
        x = _scatter(x1, mod, slot.T, yo, geo, split=(li == depth - 1))

        ctx = lambda p, lo, hi: p[:t_ctx, lo:hi]
        new["a_k"].append(ctx(pa, 256, 384).reshape(batch, seq, 2, HEAD_DIM))
        new["a_v"].append(ctx(pa, 384, 512).reshape(batch, seq, 2, HEAD_DIM))
        new["b_k"].append(ctx(pb, 256, 384).reshape(batch, seq, 2, HEAD_DIM))
        new["b_v"].append(ctx(pb, 384, 512).reshape(batch, seq, 2, HEAD_DIM))
        new["d_k"].append(ctx(pd, 256, 512).reshape(batch, seq, 4, 2, DIFF_QK_DIM))
        new["d_v"].append(ctx(pd, 512, 768).reshape(batch, seq, 4, HEAD_DIM))
        new["ssm"].append(ssm_new.reshape(batch, 2, SSM_HEADS, SSM_HEAD_DIM, SSM_STATE))

    st = lambda k: jnp.stack(new[k], axis=1)
    return (x[0].reshape(batch, seq, d), x[1].reshape(dec_batch, r, d),
            st("a_k"), st("a_v"), st("b_k"), st("b_v"), st("ssm"), st("d_k"), st("d_v"))
```

```python
import functools
import math

import numpy as np
import jax
import jax.numpy as jnp
from jax import lax
from jax.experimental import pallas as pl
from jax.experimental.pallas import tpu as pltpu

F32 = jnp.float32
BF16 = jnp.bfloat16
I32 = jnp.int32

HEAD_DIM = 64
DIFF_QK_DIM = 32
GRID_W = 64
ROPE_THETA = 10000.0
NORM_EPS = 1e-6
WINDOW = 128
SSM_CHUNK = 128
SSM_HEADS = 4
SSM_HEAD_DIM = 64
SSM_STATE = 64
SSM_CONV = 5
EC_FACTOR = 2
BRANCH_W = 256
N_BRANCH = 4

LANES = 128
VMEM_LIMIT = 56 * 1024 * 1024

COL_A = (0, 512)
COL_B = (512, 1024)
COL_C = (1024, 1920)
COL_D = (1920, 2688)
W_MIX = 2688
N_MIX_SRC = 2568
DT_SRC = 1800


def _dot(a, b):
    return jnp.dot(a, b, preferred_element_type=F32)


def _dot_nt(a, b):
    return lax.dot_general(a, b, (((1,), (1,)), ((), ())), preferred_element_type=F32)


def _dot_tn(a, b):
    return lax.dot_general(a, b, (((0,), (0,)), ((), ())), preferred_element_type=F32)


def _split2(x):
    hi = x.astype(BF16)
    lo = (x - hi.astype(F32)).astype(BF16)
    return hi, lo


def _split3(x):
    hi = x.astype(BF16)
    r = x - hi.astype(F32)
    mid = r.astype(BF16)
    lo = (r - mid.astype(F32)).astype(BF16)
    return hi, mid, lo


def _silu(x):
    return x * jax.nn.sigmoid(x)


def _rms(x, g):
    ms = jnp.mean(x * x, axis=-1, keepdims=True)
    return x * lax.rsqrt(ms + NORM_EPS) * g


def _params(sem):
    return pltpu.CompilerParams(dimension_semantics=sem, vmem_limit_bytes=VMEM_LIMIT)


def _adaln_kernel(c_ref, w_ref, b_ref, o_ref):
    s_hi, s_lo = _split2(_silu(c_ref[...]))
    w_hi, w_lo = _split2(w_ref[0])
    o_ref[0] = _dot(s_hi, w_hi) + _dot(s_hi, w_lo) + _dot(s_lo, w_hi) + b_ref[0]


def _adaln(cvec, w_ada, b_ada):
    depth, d, n = w_ada.shape
    m = cvec.shape[0]
    tn = 1024
    return pl.pallas_call(
        _adaln_kernel,
        grid=(depth, n // tn),
        in_specs=[pl.BlockSpec((m, d), lambda l, j: (0, 0)),
                  pl.BlockSpec((1, d, tn), lambda l, j: (l, 0, j)),
                  pl.BlockSpec((1, 1, tn), lambda l, j: (l, 0, j))],
        out_specs=pl.BlockSpec((1, m, tn), lambda l, j: (l, 0, j)),
        out_shape=jax.ShapeDtypeStruct((depth, m, n), F32),
        compiler_params=_params(("arbitrary", "arbitrary")),
    )(cvec, w_ada, b_ada.reshape(depth, 1, n))


def _qknorm_rope_store(p, bd, gain, cos_ref, sin_ref, segs, gsize, n, o_ref):
    parts = []
    done = 0
    for (lo, hi) in segs:
        ps = p[:, lo:hi]
        ssum = _dot((ps * ps).astype(BF16), bd[0:hi - lo, 0:hi - lo])
        parts.append(ps * lax.rsqrt(ssum * (1.0 / gsize) + NORM_EPS) * gain[:, lo:hi])
        done = hi
    parts.append(p[:, done:])
    y = jnp.concatenate(parts, axis=1)
    for c in range(p.shape[1] // LANES):
        sl = slice(c * LANES, (c + 1) * LANES)
        ys = y[:, sl]
        up = pltpu.roll(ys, LANES - n, 1)
        dn = pltpu.roll(ys, n, 1)
        l128 = lax.broadcasted_iota(I32, ys.shape, 1)
        sw = jnp.where((l128 & (2 * n - 1)) < n, up, dn)
        o_ref[:, sl] = ys * cos_ref[:, sl] + sw * sin_ref[:, sl]


def _pick_x(nct, xc_ref, xl_ref):
    return jnp.where(pl.program_id(0) < nct, xc_ref[...], xl_ref[...])


def _x_specs(nct, tm, d):
    return [pl.BlockSpec((tm, d), lambda i: (jnp.minimum(i, nct - 1), 0)),
            pl.BlockSpec((tm, d), lambda i: (jnp.maximum(i - nct, 0), 0))]


def _inproj_kernel(nct, xc_ref, xl_ref, mod_ref, g1_ref, w_ref, bda_ref, bdd_ref, ga_ref, gb_ref, gd_ref,
                   cosa_ref, sina_ref, cosd_ref, sind_ref, pa_ref, pb_ref, pc_ref, pd_ref):
    m = mod_ref[0]
    h = _rms(_pick_x(nct, xc_ref, xl_ref), g1_ref[...]) * (1.0 + m[1:2]) + m[0:1]
    hb = h.astype(BF16)
    pa = _dot(hb, w_ref[:, COL_A[0]:COL_A[1]])
    _qknorm_rope_store(pa, bda_ref[...], ga_ref[...], cosa_ref, sina_ref, ((0, 256), (256, 384)), HEAD_DIM, HEAD_DIM // 4, pa_ref)
    pb = _dot(hb, w_ref[:, COL_B[0]:COL_B[1]])
    _qknorm_rope_store(pb, bda_ref[...], gb_ref[...], cosa_ref, sina_ref, ((0, 256), (256, 384)), HEAD_DIM, HEAD_DIM // 4, pb_ref)
    pc_ref[...] = _dot(hb, w_ref[:, COL_C[0]:COL_C[1]])
    pd = _dot(hb, w_ref[:, COL_D[0]:COL_D[1]])
    _qknorm_rope_store(pd, bdd_ref[...], gd_ref[...], cosd_ref, sind_ref, ((0, 256), (256, 512)), DIFF_QK_DIM, DIFF_QK_DIM // 4, pd_ref)


def _inproj(x, mod, g1, w_mix, consts, geo, tm):
    d = x[0].shape[1]
    t_all = x[0].shape[0] + x[1].shape[0]
    nct = geo["t_ctx"] // tm
    tps = geo["r"] // tm

    def mod_row(i):
        return jnp.where(i < nct, 0, 1 + (i - nct) // tps)

    def rope_row(i):
        return jnp.where(i < nct, tps, (i - nct) % tps)

    full = lambda a: pl.BlockSpec(a.shape, lambda i: (0,) * a.ndim, pipeline_mode=pl.Buffered(1))
    widths = [COL_A[1] - COL_A[0], COL_B[1] - COL_B[0], COL_C[1] - COL_C[0], COL_D[1] - COL_D[0]]
    tab = lambda w: pl.BlockSpec((tm, w), lambda i: (rope_row(i), 0))
    return pl.pallas_call(
        functools.partial(_inproj_kernel, nct),
        grid=(t_all // tm,),
        in_specs=_x_specs(nct, tm, d) + [
                  pl.BlockSpec((1, 6, d), lambda i: (mod_row(i), 0, 0)),
                  full(g1), full(w_mix), full(consts["bda"]), full(consts["bdd"]),
                  full(consts["ga"]), full(consts["gb"]), full(consts["gd"]),
                  tab(512), tab(512), tab(768), tab(768)],
        out_specs=[pl.BlockSpec((tm, w), lambda i: (i, 0)) for w in widths],
        out_shape=[jax.ShapeDtypeStruct((t_all, w), F32) for w in widths],
        compiler_params=_params(("arbitrary",)),
    )(x[0], x[1], mod, g1, w_mix, consts["bda"], consts["bdd"], consts["ga"], consts["gb"], consts["gd"],
      consts["cosa"], consts["sina"], consts["cosd"], consts["sind"])


def _values_with_ones(v):
    lane = lax.broadcasted_iota(I32, (v.shape[0], LANES), 1)
    slabs = []
    for c in range(v.shape[1] // LANES):
        vs = v[:, c * LANES:(c + 1) * LANES]
        slabs.append(jnp.where(lane < HEAD_DIM, vs, 1.0))
        slabs.append(jnp.where(lane < HEAD_DIM, pltpu.roll(vs, HEAD_DIM, 1), 1.0))
    return jnp.concatenate(slabs, axis=1).astype(BF16)


def _attn_kernel(kind, geo, li, p_ref, ck_ref, cv_ref, sink_ref, lam_ref, gdo_ref, y_ref, kb_ref, vb_ref):
    r, past, tq, ng = geo["r"], geo["past"], geo["tq"], geo["n_ctx_groups"]
    s = pl.program_id(0)
    qi = pl.program_id(1)
    is_ctx = s < ng
    diff = kind == "D"
    kcols = (256, 512) if diff else (256, 384)
    vcols = (512, 768) if diff else (384, 512)
    wk = kcols[1] - kcols[0]
    dqk = DIFF_QK_DIM if diff else HEAD_DIM
    scale = 1.0 / math.sqrt(dqk)

    @pl.when(qi == 0)
    def _prep():
        kb_ref[0:r, :] = p_ref[:, kcols[0]:kcols[1]].astype(BF16)
        vb_ref[0:r, :] = _values_with_ones(p_ref[:, vcols[0]:vcols[1]])

        @pl.when(jnp.logical_not(is_ctx))
        def _cache():
            kb_ref[r:r + past, :] = ck_ref[0, 0].astype(BF16)
            vb_ref[r:r + past, :] = _values_with_ones(cv_ref[0, 0])

    q0 = pl.multiple_of(qi * tq, tq)
    q = p_ref[pl.ds(q0, tq), 0:256] * scale
    lane256 = lax.broadcasted_iota(I32, (tq, 256), 1)
    lane128 = lax.broadcasted_iota(I32, (tq, LANES), 1)

    def q_for(u):
        if diff:
            slab = q[:, (u // 4) * LANES:(u // 4 + 1) * LANES]
            lo = (u % 4) * DIFF_QK_DIM
            keep = (lane128 >= lo) & (lane128 < lo + DIFF_QK_DIM)
            return jnp.where(keep, slab, 0.0).astype(BF16), u // 4
        slab = q[:, (u // 2) * LANES:(u // 2 + 1) * LANES]
        g = u // 2
        if (u % 2) != g:
            slab = pltpu.roll(slab, HEAD_DIM, 1)
        keep = (lane128 < HEAD_DIM) if g == 0 else (lane128 >= HEAD_DIM)
        return jnp.where(keep, slab, 0.0).astype(BF16), 0

    def scores(u, srcs, sink):
        qm, kslab = q_for(u)
        ss = []
        for (st, n, mask) in srcs:
            kk = kb_ref[pl.ds(st, n), kslab * LANES:(kslab + 1) * LANES]
            sc = _dot_nt(qm, kk)
            if mask is not None:
                sc = jnp.where(mask, sc, -jnp.inf)
            ss.append(sc)
        mx = ss[0].max(axis=-1, keepdims=True)
        for sc in ss[1:]:
            mx = jnp.maximum(mx, sc.max(axis=-1, keepdims=True))
        if sink is not None:
            mx = jnp.maximum(mx, sink)
        return ss, mx

    def softmax_pv(ss, mx, vslab, srcs, sink, upper):
        ol = jnp.zeros((tq, LANES), F32)
        for (st, n, _), sc in zip(srcs, ss):
            pexp = jnp.exp((sc - mx).astype(BF16))
            ol = ol + _dot(pexp, vb_ref[pl.ds(st, n), vslab * LANES:(vslab + 1) * LANES])
        if sink is not None:
            ol = ol + jnp.where(lane128 >= HEAD_DIM, jnp.exp(sink - mx), 0.0)
        sw = pltpu.roll(ol, HEAD_DIM, 1)
        return sw / ol if upper else ol / sw

    if diff:
        lam_init = 0.8 - 0.6 * math.exp(-0.3 * li)
        lv = lam_ref[...]
        lam = (jnp.exp(jnp.sum(lv[0:1] * lv[1:2], axis=1, keepdims=True))
               - jnp.exp(jnp.sum(lv[2:3] * lv[3:4], axis=1, keepdims=True)) + lam_init)

    def run(srcs, batched):
        n_u = 8 if diff else 4
        sink_of = lambda u: sink_ref[li, u] if kind == "B" else None
        outs = []
        ahead = n_u if batched else 2
        pending = [scores(u, srcs, sink_of(u)) for u in range(ahead)]
        for u in range(n_u):
            if u + ahead < n_u:
                pending.append(scores(u + ahead, srcs, sink_of(u + ahead)))
            h = u // 2 if diff else u
            outs.append(softmax_pv(*pending.pop(0), h if diff else h // 2, srcs, sink_of(u), h % 2 == 1))
        halves = [outs[2 * h] - lam * outs[2 * h + 1] for h in range(4)] if diff else outs
        low = lane128 < HEAD_DIM
        acc = jnp.concatenate([jnp.where(low, halves[0], halves[1]), jnp.where(low, halves[2], halves[3])], axis=1)
        if diff:
            sq = acc * acc
            inv = jnp.zeros((tq, 256), F32)
            for h in range(4):
                head = (lane256 >= h * HEAD_DIM) & (lane256 < (h + 1) * HEAD_DIM)
                ms = jnp.sum(jnp.where(head, sq, 0.0), axis=-1, keepdims=True) * (1.0 / HEAD_DIM)
                inv = jnp.where(head, lax.rsqrt(ms + NORM_EPS), inv)
            acc = acc * inv * gdo_ref[...] * (1.0 - lam_init)
        y_ref[...] = acc

    @pl.when(is_ctx)
    def _ctx():
        run([(q0, tq, None)], True)

    @pl.when(jnp.logical_not(is_ctx))
    def _lat():
        if kind == "B":
            span = tq + 2 * WINDOW
            st = pl.multiple_of(jnp.clip(q0 - WINDOW, 0, r - span), WINDOW)
            qpos = q0 + lax.broadcasted_iota(I32, (tq, span), 0)
            kpos = st + lax.broadcasted_iota(I32, (tq, span), 1)
            ok = jnp.abs(qpos - kpos) <= WINDOW
            run([(st, span, ok), (r, past, None)], False)
        else:
            run([(0, r + past, None)], False)


def _attention(kind, geo, li, p, ck, cv, sink, lam, gdo):
    t_all, w = p.shape
    r, past, tq, ng = geo["r"], geo["past"], geo["tq"], geo["n_ctx_groups"]
    ns = t_all // r
    wk = 256 if kind == "D" else 128
    cache_idx = lambda s, q: (jnp.maximum(s - ng, 0), li, 0, 0)
    return pl.pallas_call(
        functools.partial(_attn_kernel, kind, geo, li),
        grid=(ns, r // tq),
        in_specs=[pl.BlockSpec((r, w), lambda s, q: (s, 0)),
                  pl.BlockSpec((1, 1, past, ck.shape[-1]), cache_idx),
                  pl.BlockSpec((1, 1, past, cv.shape[-1]), cache_idx),
                  pl.BlockSpec(memory_space=pltpu.SMEM),
                  pl.BlockSpec(lam.shape, lambda s, q: (0, 0)),
                  pl.BlockSpec(gdo.shape, lambda s, q: (0, 0))],
        out_specs=pl.BlockSpec((tq, 256), lambda s, q: (s * (r // tq) + q, 0)),
        out_shape=jax.ShapeDtypeStruct((t_all, 256), F32),
        scratch_shapes=[pltpu.VMEM((r + past, wk), BF16), pltpu.VMEM((r + past, 2 * wk), BF16)],
        compiler_params=_params(("arbitrary", "arbitrary")),
    )(p, ck, cv, sink, lam, gdo)


def _ssd_kernel(geo, li, p_ref, st0_ref, cw_ref, cb_ref, dtb_ref, alog_ref, dsk_ref, gs_ref,
                y_ref, so_ref, xpad, u_scr, dt_scr, la_scr, y_scr, upd_scr, cd_scr, ex_scr, st_scr, sf_ref, sb_ref):
    r, seq, ng = geo["r"], geo["seq"], geo["n_ctx_groups"]
    q = SSM_CHUNK
    nh = SSM_HEADS
    is_ctx = pl.program_id(0) < ng
    inner = nh * SSM_HEAD_DIM
    conv_ch = inner + 2 * LANES
    nchunks = r // q
    spg = r // seq
    gap = 8

    def fill(n_seq, length):
        for j in range(n_seq + 1):
            xpad[j * (length + gap):j * (length + gap) + gap, :] = jnp.zeros((gap, conv_ch), F32)
        for j in range(n_seq):
            xpad[gap + j * (length + gap):gap + j * (length + gap) + length, :] = (
                p_ref[j * length:(j + 1) * length, inner:inner + conv_ch])

    @pl.when(is_ctx)
    def _():
        fill(spg, seq)

    @pl.when(jnp.logical_not(is_ctx))
    def _():
        fill(1, r)

    a_neg = -jnp.exp(alog_ref[...])
    seq_shift = jnp.where(is_ctx, int(math.log2(seq // q)), int(math.log2(r // q)))

    def pre(c, carry):
        r0 = pl.multiple_of(c * q, q)
        st = pl.multiple_of(r0 + lax.shift_right_logical(c, seq_shift) * gap, gap)
        big = xpad[pl.ds(st, q + 16), :]
        acc = jnp.zeros((q, conv_ch), F32) + cb_ref[...]
        for j in range(SSM_CONV):
            sh = pltpu.roll(big, (q + 16) - (gap - SSM_CONV // 2 + j), 0)[0:q]
            acc = acc + sh * cw_ref[j:j + 1, :]
        u_scr[pl.ds(r0, q), :] = _silu(acc)
        raw = p_ref[pl.ds(r0, q), inner + conv_ch:inner + conv_ch + LANES] + dtb_ref[...]
        dt = jnp.maximum(raw, 0.0) + jnp.log1p(jnp.exp(-jnp.abs(raw)))
        dt_scr[pl.ds(r0, q), :] = dt
        la_scr[pl.ds(r0, q), :] = dt * a_neg
        return carry

    lax.fori_loop(0, nchunks, pre, 0)

    row_i = lax.broadcasted_iota(I32, (q, q), 0)
    col_i = lax.broadcasted_iota(I32, (q, q), 1)
    keeps = (col_i <= row_i, col_i >= row_i)
    tris = tuple(jnp.where(k, 1.0, 0.0).astype(BF16) for k in keeps)
    lane128 = lax.broadcasted_iota(I32, (q, LANES), 1)
    lane256 = lax.broadcasted_iota(I32, (q, inner), 1)
    heads256 = [(lane256 >= h * SSM_HEAD_DIM) & (lane256 < (h + 1) * SSM_HEAD_DIM) for h in range(nh)]
    lane8 = lax.broadcasted_iota(I32, (8, inner), 1)
    gmask_t = (lax.broadcasted_iota(I32, (LANES, inner), 0) // SSM_STATE
               == lax.broadcasted_iota(I32, (LANES, inner), 1) // (2 * SSM_HEAD_DIM))

    def phase1(c, carry):
        r0 = pl.multiple_of(c * q, q)
        u = u_scr[pl.ds(r0, q), :]
        xs, bm, cm = u[:, 0:inner], u[:, inner:inner + LANES], u[:, inner + LANES:inner + 2 * LANES]
        bm_b = bm.astype(BF16)
        cb = [_dot_nt(jnp.where((lane128 // SSM_STATE) == g, cm, 0.0).astype(BF16), bm_b) for g in range(2)]
        a1, a2, a3 = _split3(la_scr[pl.ds(r0, q), :])
        cums = [_dot(t, a1) + _dot(t, a2) + _dot(t, a3) for t in tris]
        cum = jnp.where(lane128 < nh, cums[0], cums[1])
        pack = jnp.where(lane128 < 2 * nh, cum, pltpu.roll(dt_scr[pl.ds(r0, q), :], 2 * nh, 1))
        pack_t = pack.T
        bm_t = bm.T
        xstack = jnp.concatenate([jnp.where(heads256[h], xs, 0.0).astype(BF16) for h in range(nh)], axis=0)
        ms, bws = [], []
        for d in range(2):
            edge = q - 1 if d == 0 else 0
            m_h, bw_h, ecol, cdx = [], [], [], jnp.zeros((8, inner), F32)
            for h in range(nh):
                c_ = d * nh + h
                colb = jnp.broadcast_to(cum[:, c_:c_ + 1], (q, q))
                crow = pack_t[c_:c_ + 1, :]
                dtrow = pack_t[2 * nh + c_:2 * nh + c_ + 1, :]
                lm = jnp.exp(jnp.where(keeps[d], colb - crow, -jnp.inf))
                m_h.append((cb[h // 2] * lm * dtrow).astype(BF16))
                tot = crow[:, edge:edge + 1]
                bw_h.append((bm_t * (dtrow * jnp.exp(tot - crow))).astype(BF16))
                ecol.append(jnp.exp(colb))
                cdx = jnp.where((lane8 >= h * SSM_HEAD_DIM) & (lane8 < (h + 1) * SSM_HEAD_DIM), jnp.exp(tot), cdx)
            ms.append(jnp.concatenate(m_h, axis=1))
            bws.append(jnp.concatenate(bw_h, axis=1))
            ex_scr[d, pl.ds(r0, q), :] = jnp.concatenate(
                [jnp.where(lane128 < SSM_HEAD_DIM, ecol[0], ecol[1]),
                 jnp.where(lane128 < SSM_HEAD_DIM, ecol[2], ecol[3])], axis=1)
            cd_scr[d, c] = cdx
        res = _dot(jnp.concatenate(ms + bws, axis=0), xstack)
        y_scr[pl.ds(r0, q), :] = res[0:q] + res[q:2 * q]
        upd_scr[0, c] = jnp.where(gmask_t, res[2 * q:3 * q], 0.0)
        upd_scr[1, c] = jnp.where(gmask_t, res[3 * q:4 * q], 0.0)
        return carry

    lax.fori_loop(0, nchunks, phase1, 0, unroll=2)

    srefs = (sf_ref, sb_ref)

    def scan_step(d, c):
        s_in = srefs[d][...]
        st_scr[d, c] = s_in.astype(BF16)
        srefs[d][...] = s_in * cd_scr[d, c][0:1, :] + upd_scr[d, c]

    @pl.when(is_ctx)
    def _ctx():
        cps = seq // q

        def one_seq(j, carry):
            for d in range(2):
                srefs[d][...] = jnp.zeros((LANES, inner), F32)
                for ci in range(cps):
                    scan_step(d, j * cps + (ci if d == 0 else cps - 1 - ci))
                sv = srefs[d][...]
                so_ref[j, d, 0:LANES, :] = sv[:, 0:LANES].T[:, 0:SSM_STATE]
                so_ref[j, d, LANES:inner, :] = pltpu.roll(sv[:, LANES:inner].T, SSM_STATE, 1)[:, 0:SSM_STATE]
            return carry

        lax.fori_loop(0, spg, one_seq, 0)

    @pl.when(jnp.logical_not(is_ctx))
    def _lat():
        for d in range(2):
            srefs[d][...] = st0_ref[0, 0, d]

        def body(i, carry):
            scan_step(0, i)
            scan_step(1, nchunks - 1 - i)
            return carry

        lax.fori_loop(0, nchunks, body, 0)

    def fin(c, carry):
        r0 = pl.multiple_of(c * q, q)
        cm_b = u_scr[pl.ds(r0, q), inner + LANES:inner + 2 * LANES].astype(BF16)
        y = y_scr[pl.ds(r0, q), :] + u_scr[pl.ds(r0, q), 0:inner] * dsk_ref[...]
        for d in range(2):
            y = y + _dot(cm_b, st_scr[d, c]) * ex_scr[d, pl.ds(r0, q), :]
        y = y * _silu(p_ref[pl.ds(r0, q), 0:inner])
        y_ref[pl.ds(r0, q), :] = _rms(y, gs_ref[...])
        return carry

    lax.fori_loop(0, nchunks, fin, 0, unroll=4)


def _ssd(geo, li, pc, st0, cw, cb, dtb, alog, dsk, gs):
    t_all, w = pc.shape
    r, seq, ng = geo["r"], geo["seq"], geo["n_ctx_groups"]
    ns = t_all // r
    spg = r // seq
    nchunks = r // SSM_CHUNK
    inner = SSM_HEADS * SSM_HEAD_DIM
    full = lambda a: pl.BlockSpec(a.shape, lambda s: (0,) * a.ndim)
    return pl.pallas_call(
        functools.partial(_ssd_kernel, geo, li),
        grid=(ns,),
        in_specs=[pl.BlockSpec((r, w), lambda s: (s, 0)),
                  pl.BlockSpec((1, 1, 2, LANES, inner), lambda s: (jnp.maximum(s - ng, 0), li, 0, 0, 0)),
                  full(cw), full(cb), full(dtb), full(alog), full(dsk), full(gs)],
        out_specs=[pl.BlockSpec((r, inner), lambda s: (s, 0)),
                   pl.BlockSpec((spg, 2, inner, SSM_STATE), lambda s: (jnp.minimum(s, ng - 1), 0, 0, 0))],
        out_shape=[jax.ShapeDtypeStruct((t_all, inner), F32),
                   jax.ShapeDtypeStruct((ng * spg, 2, inner, SSM_STATE), F32)],
        scratch_shapes=[pltpu.VMEM((r + 8 * (spg + 1), 2 * inner), F32), pltpu.VMEM((r, 2 * inner), F32),
                        pltpu.VMEM((r, LANES), F32), pltpu.VMEM((r, LANES), F32),
                        pltpu.VMEM((r, inner), F32),
                        pltpu.VMEM((2, nchunks, LANES, inner), F32), pltpu.VMEM((2, nchunks, 8, inner), F32),
                        pltpu.VMEM((2, r, inner), F32), pltpu.VMEM((2, nchunks, LANES, inner), BF16),
                        pltpu.VMEM((LANES, inner), F32), pltpu.VMEM((LANES, inner), F32)],
        compiler_params=_params(("arbitrary",)),
    )(pc, st0, cw, cb, dtb, alog, dsk, gs)


def _merge_kernel(nct, xc_ref, xl_ref, mod_ref, g1_ref, g2_ref, ya_ref, yb_ref, yc_ref, yd_ref, wg_ref, wbr_ref,
                  wo_ref, wr_ref, x1_ref, h2_ref, aff_ref):
    d = xc_ref.shape[1]
    x = _pick_x(nct, xc_ref, xl_ref)
    m = mod_ref[0]
    hb = (_rms(x, g1_ref[...]) * (1.0 + m[1:2]) + m[0:1]).astype(BF16)
    merged = jnp.zeros(x.shape, F32)
    for n, y_ref in enumerate((ya_ref, yb_ref, yc_ref, yd_ref)):
        gate = jax.nn.sigmoid(_dot(hb, wg_ref[:, n * d:(n + 1) * d]))
        merged = merged + gate * _dot(y_ref[...].astype(BF16), wbr_ref[n])
    x1 = x + m[2:3] * _dot(merged.astype(BF16), wo_ref[...])
    x1_ref[...] = x1
    h2 = _rms(x1, g2_ref[...]) * (1.0 + m[4:5]) + m[3:4]
    h2_ref[...] = h2.astype(BF16)
    h_hi, h_lo = _split2(h2)
    w_hi, w_lo = _split2(wr_ref[...])
    logit = _dot_nt(w_hi, h_hi) + _dot_nt(w_hi, h_lo) + _dot_nt(w_lo, h_hi)
    e = jnp.exp(logit - logit.max(axis=0, keepdims=True))
    aff_ref[...] = e / e.sum(axis=0, keepdims=True)


def _merge(x, mod, g1, g2, ys, wg, wbr, wo, wr_t, geo, tm):
    d = x[0].shape[1]
    t_all = x[0].shape[0] + x[1].shape[0]
    ne = wr_t.shape[0]
    nct = geo["t_ctx"] // tm
    tps = geo["r"] // tm
    mod_row = lambda i: jnp.where(i < nct, 0, 1 + (i - nct) // tps)
    full = lambda a: pl.BlockSpec(a.shape, lambda i: (0,) * a.ndim, pipeline_mode=pl.Buffered(1))
    ytile = pl.BlockSpec((tm, BRANCH_W), lambda i: (i, 0))
    return pl.pallas_call(
        functools.partial(_merge_kernel, nct),
        grid=(t_all // tm,),
        in_specs=_x_specs(nct, tm, d) + [
                  pl.BlockSpec((1, 6, d), lambda i: (mod_row(i), 0, 0)),
                  full(g1), full(g2), ytile, ytile, ytile, ytile, full(wg), full(wbr), full(wo), full(wr_t)],
        out_specs=[pl.BlockSpec((tm, d), lambda i: (i, 0)),
                   pl.BlockSpec((tm, d), lambda i: (i, 0)),
                   pl.BlockSpec((ne, tm), lambda i: (0, i))],
        out_shape=[jax.ShapeDtypeStruct((t_all, d), F32),
                   jax.ShapeDtypeStruct((t_all, d), BF16),
                   jax.ShapeDtypeStruct((ne, t_all), F32)],
        compiler_params=_params(("arbitrary",)),
    )(x[0], x[1], mod, g1, g2, *ys, wg, wbr, wo, wr_t)


def _kth_largest_bits(a, cap):
    def body(i, thr):
        cand = thr | lax.shift_left(jnp.int32(1), 30 - i)
        cnt = jnp.sum(jnp.where(a >= lax.bitcast_convert_type(cand, F32), 1.0, 0.0), axis=1, keepdims=True)
        return jnp.where(cnt >= cap, cand, thr)
    return lax.fori_loop(0, 31, body, jnp.zeros((a.shape[0], 1), I32))


def _route_kernel(geo, aff_ref, slot_ref):
    r, seq, ng = geo["r"], geo["seq"], geo["n_ctx_groups"]
    ne = aff_ref.shape[0]
    blk = 256
    s = pl.program_id(0)
    upper = jnp.where(lax.broadcasted_iota(I32, (blk, blk), 0) < lax.broadcasted_iota(I32, (blk, blk), 1),
                      1.0, 0.0).astype(BF16)

    def select(a, cap, nblk):
        thr = _kth_largest_bits(a, cap)
        gt = a >= lax.bitcast_convert_type(thr + 1, F32)
        eq = (a >= lax.bitcast_convert_type(thr, F32)) & jnp.logical_not(gt)
        need = cap - jnp.sum(jnp.where(gt, 1.0, 0.0), axis=1, keepdims=True)
        eq_f = jnp.where(eq, 1.0, 0.0)
        outs = []
        carry_e = jnp.zeros((a.shape[0], 1), F32)
        carry_s = jnp.zeros((a.shape[0], 1), F32)
        for b in range(nblk):
            sl = slice(b * blk, (b + 1) * blk)
            pe = _dot(eq_f[:, sl].astype(BF16), upper) + carry_e
            sel = gt[:, sl] | (eq[:, sl] & (pe < need))
            sel_f = jnp.where(sel, 1.0, 0.0)
            ps = _dot(sel_f.astype(BF16), upper) + carry_s
            outs.append(jnp.where(sel, ps, -1.0))
            carry_e = carry_e + eq_f[:, sl].sum(axis=1, keepdims=True)
            carry_s = carry_s + sel_f.sum(axis=1, keepdims=True)
        return outs

    @pl.when(s < ng)
    def _ctx():
        spg = r // seq
        cap = EC_FACTOR * seq // ne
        a = jnp.concatenate([aff_ref[:, j * seq:(j + 1) * seq] for j in range(spg)], axis=0)
        out = select(a, cap, seq // blk)
        out = out[0] if len(out) == 1 else jnp.concatenate(out, axis=1)
        for j in range(spg):
            o = out[j * ne:(j + 1) * ne]
            slot_ref[:, j * seq:(j + 1) * seq] = jnp.where(o >= 0, o + j * cap, -1.0).astype(I32)

    @pl.when(s >= ng)
    def _lat():
        cap = EC_FACTOR * r // ne
        out = select(aff_ref[...], cap, r // blk)
        for b, o in enumerate(out):
            slot_ref[:, b * blk:(b + 1) * blk] = o.astype(I32)


def _route(aff_t, geo):
    ne, t_all = aff_t.shape
    r = geo["r"]
    return pl.pallas_call(
        functools.partial(_route_kernel, geo),
        grid=(t_all // r,),
        in_specs=[pl.BlockSpec((ne, r), lambda s: (0, s))],
        out_specs=pl.BlockSpec((ne, r), lambda s: (0, s)),
        out_shape=jax.ShapeDtypeStruct((ne, t_all), I32),
        compiler_params=_params(("arbitrary",)),
    )(aff_t)


def _expert_kernel(x_ref, slot_ref, aff_ref, w1_ref, w3_ref, w2_ref, yo_ref, wb1, wb3, wb2):
    @pl.when(pl.program_id(1) == 0)
    def _cast():
        wb1[...] = w1_ref[0, 0].astype(BF16)
        wb3[...] = w3_ref[0, 0].astype(BF16)
        wb2[...] = w2_ref[0, 0].astype(BF16)

    cap = yo_ref.shape[2]
    slot = slot_ref[0]
    pick = lax.broadcasted_iota(I32, (cap, slot.shape[1]), 0) == slot
    xs = _dot(jnp.where(pick, 1.0, 0.0).astype(BF16), x_ref[...]).astype(BF16)
    gs = jnp.sum(jnp.where(pick, aff_ref[0], 0.0), axis=1, keepdims=True)
    hid = _silu(_dot(xs, wb1[...])) * _dot(xs, wb3[...])
    yo_ref[0, 0] = (_dot(hid.astype(BF16), wb2[...]) * gs).astype(BF16)


def _experts(h2, slot3, aff3, w1, w3, w2, geo, li):
    t_all, d = h2.shape
    _, ne, _, f = w1.shape
    r = geo["r"]
    ns = t_all // r
    cap = EC_FACTOR * r // ne
    wspec = lambda a: pl.BlockSpec((1, 1) + a.shape[2:], lambda e, s: (li, e, 0, 0))
    return pl.pallas_call(
        _expert_kernel,
        grid=(ne, ns),
        in_specs=[pl.BlockSpec((r, d), lambda e, s: (s, 0)),
                  pl.BlockSpec((1, 1, r), lambda e, s: (e, 0, s)),
                  pl.BlockSpec((1, 1, r), lambda e, s: (e, 0, s)),
                  wspec(w1), wspec(w3), wspec(w2)],
        out_specs=pl.BlockSpec((1, 1, cap, d), lambda e, s: (s, e, 0, 0)),
        out_shape=jax.ShapeDtypeStruct((ns, ne, cap, d), BF16),
        scratch_shapes=[pltpu.VMEM((d, f), BF16), pltpu.VMEM((d, f), BF16), pltpu.VMEM((f, d), BF16)],
        compiler_params=_params(("arbitrary", "arbitrary")),
    )(h2, slot3, aff3, w1, w3, w2)


def _scatter_kernel(ng, x_ref, mod_ref, slot_ref, yo_ref, oc_ref, ol_ref):
    tt = x_ref.shape[0]
    _, ne, cap, d = yo_ref.shape
    slot = slot_ref[...]
    lane = lax.broadcasted_iota(I32, (tt, cap), 1)
    onehot = jnp.concatenate([jnp.where(slot[:, e:e + 1] == lane, 1.0, 0.0).astype(BF16) for e in range(ne)],
                             axis=1)
    ffn = _dot(onehot, yo_ref[0].reshape(ne * cap, d))
    out = x_ref[...] + mod_ref[0][5:6] * ffn
    is_ctx = pl.program_id(0) < ng

    @pl.when(is_ctx)
    def _():
        oc_ref[...] = out

    @pl.when(jnp.logical_not(is_ctx))
    def _():
        ol_ref[...] = out


def _scatter(x1, mod, slot_t, yo, geo, tt=512):
    t_all, d = x1.shape
    ns, ne, cap, _ = yo.shape
    r = geo["r"]
    ng = geo["n_ctx_groups"]
    tpg = r // tt
    nct = ng * tpg
    mod_row = lambda s, i: jnp.where(s < ng, 0, 1 + s - ng)
    out_specs = [pl.BlockSpec((tt, d), lambda s, i: (jnp.minimum(s * tpg + i, nct - 1), 0)),
                 pl.BlockSpec((tt, d), lambda s, i: (jnp.maximum(s * tpg + i - nct, 0), 0))]
    out_shape = [jax.ShapeDtypeStruct((nct * tt, d), F32), jax.ShapeDtypeStruct((t_all - nct * tt, d), F32)]
    return pl.pallas_call(
        functools.partial(_scatter_kernel, ng),
        grid=(ns, tpg),
        in_specs=[pl.BlockSpec((tt, d), lambda s, i: (s * tpg + i, 0)),
                  pl.BlockSpec((1, 6, d), lambda s, i: (mod_row(s, i), 0, 0)),
                  pl.BlockSpec((tt, ne), lambda s, i: (s * tpg + i, 0)),
                  pl.BlockSpec((1, ne, cap, d), lambda s, i: (s, 0, 0, 0))],
        out_specs=out_specs,
        out_shape=out_shape,
        compiler_params=_params(("arbitrary", "arbitrary")),
    )(x1, mod, slot_t, yo)


def _rope_table(s, d, n_qk_lanes, n_v_lanes, pad_rows):
    n = d // 4
    t = np.arange(s)
    rows = (t // GRID_W).astype(np.float64)
    cols = (t % GRID_W).astype(np.float64)
    inv = ROPE_THETA ** (-np.arange(n, dtype=np.float64) / n)
    ang = np.stack([rows[:, None] * inv, cols[:, None] * inv], axis=1)
    cos, sin = np.cos(ang), np.sin(ang)
    cos_h = np.concatenate([cos[:, 0], cos[:, 0], cos[:, 1], cos[:, 1]], axis=-1)
    sin_h = np.concatenate([-sin[:, 0], sin[:, 0], -sin[:, 1], sin[:, 1]], axis=-1)
    reps = n_qk_lanes // d
    cos_f = np.concatenate([np.tile(cos_h, (1, reps)), np.ones((s, n_v_lanes))], axis=1)
    sin_f = np.concatenate([np.tile(sin_h, (1, reps)), np.zeros((s, n_v_lanes))], axis=1)
    w = n_qk_lanes + n_v_lanes
    cos_f = np.concatenate([cos_f, np.ones((pad_rows, w))], axis=0)
    sin_f = np.concatenate([sin_f, np.zeros((pad_rows, w))], axis=0)
    return jnp.asarray(cos_f.astype(np.float32)), jnp.asarray(sin_f.astype(np.float32))


def _block_diag_ones(w, gsize):
    i = np.arange(w)
    return jnp.asarray((i[:, None] // gsize == i[None, :] // gsize).astype(np.float32), dtype=BF16)


def _pad_lanes(a, w):
    return jnp.pad(a, ((0, 0), (0, w - a.shape[1])))


def kernel(x_prompt, x_sample, c, cache_a_k, cache_a_v, cache_b_k, cache_b_v, state_ssm, cache_d_k, cache_d_v, c_ctx, w_ada, b_ada, g_norm1, g_norm2, w_in, g_qa, g_ka, g_qb, g_kb, sink_b, conv_w, conv_b, dt_bias, a_log, d_skip, g_ssm, g_qd, g_kd, lam_q1, lam_k1, lam_q2, lam_k2, g_dout, w_br, w_out, w_router, w_e1, w_e3, w_e2):
    batch, seq, d = x_prompt.shape
    dec_batch, dec_seq, _ = x_sample.shape
    depth = w_in.shape[0]
    past = cache_a_k.shape[2]
    ne = w_router.shape[2]
    r = dec_seq
    t_ctx = batch * seq
    assert t_ctx % r == 0 and r % seq == 0 and seq == 256 and past % 16 == 0
    tm = 512
    assert t_ctx % tm == 0 and r % tm == 0
    geo = dict(r=r, seq=seq, past=past, tq=256, t_ctx=t_ctx, n_ctx_groups=t_ctx // r)
    ng = geo["n_ctx_groups"]

    cosa, sina = _rope_table(r, HEAD_DIM, 384, 128, tm)
    cosd, sind = _rope_table(r, DIFF_QK_DIM, 512, 256, tm)
    bda = _block_diag_ones(256, HEAD_DIM)
    bdd = _block_diag_ones(256, DIFF_QK_DIM)
    ones = lambda n: jnp.ones((n,), F32)
    inner = SSM_HEADS * SSM_HEAD_DIM
    gsel = (np.arange(LANES)[:, None] // SSM_STATE) == (np.arange(inner)[None, :] // (2 * SSM_HEAD_DIM))

    nm = 16
    cvec = jnp.concatenate([c_ctx[None, :], c, jnp.zeros((nm - 1 - dec_batch, d), F32)], axis=0)
    mod_all = _adaln(cvec, w_ada, b_ada)

    x = (x_prompt.reshape(t_ctx, d), x_sample.reshape(dec_batch * r, d))
    ck_a = cache_a_k.reshape(dec_batch, depth, past, -1)
    cv_a = cache_a_v.reshape(dec_batch, depth, past, -1)
    ck_b = cache_b_k.reshape(dec_batch, depth, past, -1)
    cv_b = cache_b_v.reshape(dec_batch, depth, past, -1)
    ck_d = cache_d_k.reshape(dec_batch, depth, past, -1)
    cv_d = cache_d_v.reshape(dec_batch, depth, past, -1)
    st0 = jnp.moveaxis(state_ssm, -1, -3).reshape(dec_batch, depth, 2, SSM_STATE, inner)
    st0 = jnp.where(gsel, jnp.concatenate([st0, st0], axis=-2), 0.0)

    new = {k: [] for k in ("a_k", "a_v", "b_k", "b_v", "ssm", "d_k", "d_v")}
    for li in range(depth):
        mod = mod_all[li].reshape(nm, 6, d)
        wl = w_in[li]
        w_mix = jnp.concatenate([wl[:, :DT_SRC], jnp.zeros((d, COL_C[1] - COL_C[0] - 776), F32),
                                 wl[:, DT_SRC:N_MIX_SRC]], axis=1).astype(BF16)
        w_gate = wl[:, N_MIX_SRC:].astype(BF16)
        consts = dict(
            bda=bda, bdd=bdd, cosa=cosa, sina=sina, cosd=cosd, sind=sind,
            ga=jnp.concatenate([jnp.tile(g_qa[li], 4), jnp.tile(g_ka[li], 2), ones(128)])[None, :],
            gb=jnp.concatenate([jnp.tile(g_qb[li], 4), jnp.tile(g_kb[li], 2), ones(128)])[None, :],
            gd=jnp.concatenate([jnp.tile(g_qd[li], 8), jnp.tile(g_kd[li], 8), ones(256)])[None, :])
        g1 = g_norm1[li][None, :]
        g2 = g_norm2[li][None, :]
        pa, pb, pc, pd = _inproj(x, mod, g1, w_mix, consts, geo, tm)

        lam = jnp.stack([lam_q1[li], lam_k1[li], lam_q2[li], lam_k2[li]], axis=0)
        gdo = jnp.tile(g_dout[li], 4)[None, :]
        ya = _attention("A", geo, li, pa, ck_a, cv_a, sink_b, lam, gdo)
        yb = _attention("B", geo, li, pb, ck_b, cv_b, sink_b, lam, gdo)
        yd = _attention("D", geo, li, pd, ck_d, cv_d, sink_b, lam, gdo)
        yc, ssm_new = _ssd(geo, li, pc, st0,
                           jnp.pad(conv_w[li], ((0, 8 - SSM_CONV), (0, 0))), conv_b[li][None, :],
                           _pad_lanes(dt_bias[li].reshape(1, -1), LANES), _pad_lanes(a_log[li].reshape(1, -1), LANES),
                           jnp.repeat(d_skip[li], SSM_HEAD_DIM)[None, :], g_ssm[li][None, :])

        x1, h2, aff_t = _merge(x, mod, g1, g2, (ya, yb, yc, yd), w_gate, w_br[li].astype(BF16),
                               w_out[li].astype(BF16), w_router[li].T, geo, tm)
        slot = _route(aff_t, geo)
        yo = _experts(h2, slot.reshape(ne, 1, -1), aff_t.reshape(ne, 1, -1), w_e1, w_e3, w_e2, geo, li)
        x = _scatter(x1, mod, slot.T, yo, geo)

        ctx = lambda p, lo, hi: p[:t_ctx, lo:hi]
        new["a_k"].append(ctx(pa, 256, 384).reshape(batch, seq, 2, HEAD_DIM))
        new["a_v"].append(ctx(pa, 384, 512).reshape(batch, seq, 2, HEAD_DIM))
        new["b_k"].append(ctx(pb, 256, 384).reshape(batch, seq, 2, HEAD_DIM))
        new["b_v"].append(ctx(pb, 384, 512).reshape(batch, seq, 2, HEAD_DIM))
        new["d_k"].append(ctx(pd, 256, 512).reshape(batch, seq, 4, 2, DIFF_QK_DIM))
        new["d_v"].append(ctx(pd, 512, 768).reshape(batch, seq, 4, HEAD_DIM))
        new["ssm"].append(ssm_new.reshape(batch, 2, SSM_HEADS, SSM_HEAD_DIM, SSM_STATE))

    st = lambda k: jnp.stack(new[k], axis=1)
    return (x[0].reshape(batch, seq, d), x[1].reshape(dec_batch, r, d),
            st("a_k"), st("a_v"), st("b_k"), st("b_v"), st("ssm"), st("d_k"), st("d_v"))
```

```python
import functools
import math

import numpy as np
import jax
import jax.numpy as jnp
from jax import lax
from jax.experimental import pallas as pl
from jax.experimental.pallas import tpu as pltpu

F32 = jnp.float32
BF16 = jnp.bfloat16
I32 = jnp.int32

HEAD_DIM = 64
DIFF_QK_DIM = 32
GRID_W = 64
ROPE_THETA = 10000.0
NORM_EPS = 1e-6
WINDOW = 128
SSM_CHUNK = 128
SSM_HEADS = 4
SSM_HEAD_DIM = 64
SSM_STATE = 64
SSM_CONV = 5
EC_FACTOR = 2
BRANCH_W = 256
N_BRANCH = 4

ROUTE_TILE = 256
GATHER_WINDOW = 80
SCATTER_WINDOW = 128

LANES = 128
VMEM_LIMIT = 56 * 1024 * 1024

COL_A = (0, 512)
COL_B = (512, 1024)
COL_C = (1024, 1920)
COL_D = (1920, 2688)
W_MIX = 2688
N_MIX_SRC = 2568
DT_SRC = 1800


def _dot(a, b):
    return jnp.dot(a, b, preferred_element_type=F32)


def _dot_nt(a, b):
    return lax.dot_general(a, b, (((1,), (1,)), ((), ())), preferred_element_type=F32)


def _dot_tn(a, b):
    return lax.dot_general(a, b, (((0,), (0,)), ((), ())), preferred_element_type=F32)


def _split2(x):
    hi = x.astype(BF16)
    lo = (x - hi.astype(F32)).astype(BF16)
    return hi, lo


def _split3(x):
    hi = x.astype(BF16)
    r = x - hi.astype(F32)
    mid = r.astype(BF16)
    lo = (r - mid.astype(F32)).astype(BF16)
    return hi, mid, lo


def _silu(x):
    return x * jax.nn.sigmoid(x)


def _rms(x, g):
    ms = jnp.mean(x * x, axis=-1, keepdims=True)
    return x * lax.rsqrt(ms + NORM_EPS) * g


def _params(sem):
    return pltpu.CompilerParams(dimension_semantics=sem, vmem_limit_bytes=VMEM_LIMIT)


def _adaln_kernel(c_ref, w_ref, b_ref, o_ref):
    s_hi, s_lo = _split2(_silu(c_ref[...]))
    w_hi, w_lo = _split2(w_ref[0])
    o_ref[0] = _dot(s_hi, w_hi) + _dot(s_hi, w_lo) + _dot(s_lo, w_hi) + b_ref[0]


def _adaln(cvec, w_ada, b_ada):
    depth, d, n = w_ada.shape
    m = cvec.shape[0]
    tn = 1024
    return pl.pallas_call(
        _adaln_kernel,
        grid=(depth, n // tn),
        in_specs=[pl.BlockSpec((m, d), lambda l, j: (0, 0)),
                  pl.BlockSpec((1, d, tn), lambda l, j: (l, 0, j)),
                  pl.BlockSpec((1, 1, tn), lambda l, j: (l, 0, j))],
        out_specs=pl.BlockSpec((1, m, tn), lambda l, j: (l, 0, j)),
        out_shape=jax.ShapeDtypeStruct((depth, m, n), F32),
        compiler_params=_params(("arbitrary", "arbitrary")),
    )(cvec, w_ada, b_ada.reshape(depth, 1, n))


def _qknorm_rope_store(p, bd, gain, cos_ref, sin_ref, segs, gsize, n, o_ref):
    parts = []
    done = 0
    for (lo, hi) in segs:
        ps = p[:, lo:hi]
        ssum = _dot((ps * ps).astype(BF16), bd[0:hi - lo, 0:hi - lo])
        parts.append(ps * lax.rsqrt(ssum * (1.0 / gsize) + NORM_EPS) * gain[:, lo:hi])
        done = hi
    parts.append(p[:, done:])
    y = jnp.concatenate(parts, axis=1)
    for c in range(p.shape[1] // LANES):
        sl = slice(c * LANES, (c + 1) * LANES)
        ys = y[:, sl]
        up = pltpu.roll(ys, LANES - n, 1)
        dn = pltpu.roll(ys, n, 1)
        l128 = lax.broadcasted_iota(I32, ys.shape, 1)
        sw = jnp.where((l128 & (2 * n - 1)) < n, up, dn)
        o_ref[:, sl] = ys * cos_ref[:, sl] + sw * sin_ref[:, sl]


def _pick_x(nct, xc_ref, xl_ref):
    return jnp.where(pl.program_id(0) < nct, xc_ref[...], xl_ref[...])


def _x_specs(nct, tm, d):
    return [pl.BlockSpec((tm, d), lambda i: (jnp.minimum(i, nct - 1), 0)),
            pl.BlockSpec((tm, d), lambda i: (jnp.maximum(i - nct, 0), 0))]


def _inproj_kernel(nct, xc_ref, xl_ref, mod_ref, g1_ref, w_ref, bda_ref, bdd_ref, ga_ref, gb_ref, gd_ref,
                   cosa_ref, sina_ref, cosd_ref, sind_ref, pa_ref, pb_ref, pc_ref, pd_ref):
    m = mod_ref[0]
    h = _rms(_pick_x(nct, xc_ref, xl_ref), g1_ref[...]) * (1.0 + m[1:2]) + m[0:1]
    hb = h.astype(BF16)
    pa = _dot(hb, w_ref[:, COL_A[0]:COL_A[1]])
    pb = _dot(hb, w_ref[:, COL_B[0]:COL_B[1]])
    _qknorm_rope_store(pa, bda_ref[...], ga_ref[...], cosa_ref, sina_ref, ((0, 256), (256, 384)), HEAD_DIM, HEAD_DIM // 4, pa_ref)
    pd = _dot(hb, w_ref[:, COL_D[0]:COL_D[1]])
    _qknorm_rope_store(pb, bda_ref[...], gb_ref[...], cosa_ref, sina_ref, ((0, 256), (256, 384)), HEAD_DIM, HEAD_DIM // 4, pb_ref)
    pc_ref[...] = _dot(hb, w_ref[:, COL_C[0]:COL_C[1]])
    _qknorm_rope_store(pd, bdd_ref[...], gd_ref[...], cosd_ref, sind_ref, ((0, 256), (256, 512)), DIFF_QK_DIM, DIFF_QK_DIM // 4, pd_ref)


def _inproj(x, mod, g1, w_mix, consts, geo, tm):
    d = x[0].shape[1]
    t_all = x[0].shape[0] + x[1].shape[0]
    nct = geo["t_ctx"] // tm
    tps = geo["r"] // tm

    def mod_row(i):
        return jnp.where(i < nct, 0, 1 + (i - nct) // tps)

    def rope_row(i):
        return jnp.where(i < nct, tps, (i - nct) % tps)

    full = lambda a: pl.BlockSpec(a.shape, lambda i: (0,) * a.ndim, pipeline_mode=pl.Buffered(1))
    widths = [COL_A[1] - COL_A[0], COL_B[1] - COL_B[0], COL_C[1] - COL_C[0], COL_D[1] - COL_D[0]]
    tab = lambda w: pl.BlockSpec((tm, w), lambda i: (rope_row(i), 0))
    return pl.pallas_call(
        functools.partial(_inproj_kernel, nct),
        grid=(t_all // tm,),
        in_specs=_x_specs(nct, tm, d) + [
                  pl.BlockSpec((1, 6, d), lambda i: (mod_row(i), 0, 0)),
                  full(g1), full(w_mix), full(consts["bda"]), full(consts["bdd"]),
                  full(consts["ga"]), full(consts["gb"]), full(consts["gd"]),
                  tab(512), tab(512), tab(768), tab(768)],
        out_specs=[pl.BlockSpec((tm, w), lambda i: (i, 0)) for w in widths],
        out_shape=[jax.ShapeDtypeStruct((t_all, w), F32) for w in widths],
        compiler_params=_params(("arbitrary",)),
    )(x[0], x[1], mod, g1, w_mix, consts["bda"], consts["bdd"], consts["ga"], consts["gb"], consts["gd"],
      consts["cosa"], consts["sina"], consts["cosd"], consts["sind"])


def _values_with_ones(v):
    lane = lax.broadcasted_iota(I32, (v.shape[0], LANES), 1)
    slabs = []
    for c in range(v.shape[1] // LANES):
        vs = v[:, c * LANES:(c + 1) * LANES]
        slabs.append(jnp.where(lane < HEAD_DIM, vs, 1.0))
        slabs.append(jnp.where(lane < HEAD_DIM, pltpu.roll(vs, HEAD_DIM, 1), 1.0))
    return jnp.concatenate(slabs, axis=1).astype(BF16)


def _attn_kernel(kind, geo, li, p_ref, ck_ref, cv_ref, sink_ref, lam_ref, gdo_ref, y_ref, kb_ref, vb_ref):
    r, past, tq, ng = geo["r"], geo["past"], geo["tq"], geo["n_ctx_groups"]
    s = pl.program_id(0)
    qi = pl.program_id(1)
    is_ctx = s < ng
    diff = kind == "D"
    kcols = (256, 512) if diff else (256, 384)
    vcols = (512, 768) if diff else (384, 512)
    wk = kcols[1] - kcols[0]
    dqk = DIFF_QK_DIM if diff else HEAD_DIM
    scale = 1.0 / math.sqrt(dqk)

    @pl.when(qi == 0)
    def _prep():
        kb_ref[0:r, :] = p_ref[:, kcols[0]:kcols[1]].astype(BF16)
        vb_ref[0:r, :] = _values_with_ones(p_ref[:, vcols[0]:vcols[1]])

        @pl.when(jnp.logical_not(is_ctx))
        def _cache():
            kb_ref[r:r + past, :] = ck_ref[0, 0].astype(BF16)
            vb_ref[r:r + past, :] = _values_with_ones(cv_ref[0, 0])

    q0 = pl.multiple_of(qi * tq, tq)
    q = p_ref[pl.ds(q0, tq), 0:256] * scale
    lane256 = lax.broadcasted_iota(I32, (tq, 256), 1)
    lane128 = lax.broadcasted_iota(I32, (tq, LANES), 1)

    def q_for(u):
        if diff:
            slab = q[:, (u // 4) * LANES:(u // 4 + 1) * LANES]
            lo = (u % 4) * DIFF_QK_DIM
            keep = (lane128 >= lo) & (lane128 < lo + DIFF_QK_DIM)
            return jnp.where(keep, slab, 0.0).astype(BF16), u // 4
        slab = q[:, (u // 2) * LANES:(u // 2 + 1) * LANES]
        g = u // 2
        if (u % 2) != g:
            slab = pltpu.roll(slab, HEAD_DIM, 1)
        keep = (lane128 < HEAD_DIM) if g == 0 else (lane128 >= HEAD_DIM)
        return jnp.where(keep, slab, 0.0).astype(BF16), 0

    def scores(u, srcs, sink):
        qm, kslab = q_for(u)
        ss = []
        for (st, n, mask) in srcs:
            kk = kb_ref[pl.ds(st, n), kslab * LANES:(kslab + 1) * LANES]
            sc = _dot_nt(qm, kk)
            if mask is not None:
                sc = jnp.where(mask, sc, -jnp.inf)
            ss.append(sc)
        mx = ss[0].max(axis=-1, keepdims=True)
        for sc in ss[1:]:
            mx = jnp.maximum(mx, sc.max(axis=-1, keepdims=True))
        if sink is not None:
            mx = jnp.maximum(mx, sink)
        return ss, mx

    def softmax_pv(ss, mx, vslab, srcs, sink, upper):
        ol = jnp.zeros((tq, LANES), F32)
        for (st, n, _), sc in zip(srcs, ss):
            pexp = jnp.exp((sc - mx).astype(BF16))
            ol = ol + _dot(pexp, vb_ref[pl.ds(st, n), vslab * LANES:(vslab + 1) * LANES])
        if sink is not None:
            ol = ol + jnp.where(lane128 >= HEAD_DIM, jnp.exp(sink - mx), 0.0)
        sw = pltpu.roll(ol, HEAD_DIM, 1)
        return sw / ol if upper else ol / sw

    if diff:
        lam_init = 0.8 - 0.6 * math.exp(-0.3 * li)
        lv = lam_ref[...]
        lam = (jnp.exp(jnp.sum(lv[0:1] * lv[1:2], axis=1, keepdims=True))
               - jnp.exp(jnp.sum(lv[2:3] * lv[3:4], axis=1, keepdims=True)) + lam_init)

    def run(srcs, batched):
        n_u = 8 if diff else 4
        sink_of = lambda u: sink_ref[li, u] if kind == "B" else None
        outs = []
        ahead = n_u if batched else 2
        pending = [scores(u, srcs, sink_of(u)) for u in range(ahead)]
        for u in range(n_u):
            if u + ahead < n_u:
                pending.append(scores(u + ahead, srcs, sink_of(u + ahead)))
            h = u // 2 if diff else u
            outs.append(softmax_pv(*pending.pop(0), h if diff else h // 2, srcs, sink_of(u), h % 2 == 1))
        halves = [outs[2 * h] - lam * outs[2 * h + 1] for h in range(4)] if diff else outs
        low = lane128 < HEAD_DIM
        acc = jnp.concatenate([jnp.where(low, halves[0], halves[1]), jnp.where(low, halves[2], halves[3])], axis=1)
        if diff:
            sq = acc * acc
            inv = jnp.zeros((tq, 256), F32)
            for h in range(4):
                head = (lane256 >= h * HEAD_DIM) & (lane256 < (h + 1) * HEAD_DIM)
                ms = jnp.sum(jnp.where(head, sq, 0.0), axis=-1, keepdims=True) * (1.0 / HEAD_DIM)
                inv = jnp.where(head, lax.rsqrt(ms + NORM_EPS), inv)
            acc = acc * inv * gdo_ref[...] * (1.0 - lam_init)
        y_ref[...] = acc

    @pl.when(is_ctx)
    def _ctx():
        run([(q0, tq, None)], True)

    @pl.when(jnp.logical_not(is_ctx))
    def _lat():
        if kind == "B":
            span = tq + 2 * WINDOW
            st = pl.multiple_of(jnp.clip(q0 - WINDOW, 0, r - span), WINDOW)
            qpos = q0 + lax.broadcasted_iota(I32, (tq, span), 0)
            kpos = st + lax.broadcasted_iota(I32, (tq, span), 1)
            ok = jnp.abs(qpos - kpos) <= WINDOW
            run([(st, span, ok), (r, past, None)], False)
        else:
            run([(0, r + past, None)], False)


def _attention(kind, geo, li, p, ck, cv, sink, lam, gdo):
    t_all, w = p.shape
    r, past, tq, ng = geo["r"], geo["past"], geo["tq"], geo["n_ctx_groups"]
    ns = t_all // r
    wk = 256 if kind == "D" else 128
    cache_idx = lambda s, q: (jnp.maximum(s - ng, 0), li, 0, 0)
    return pl.pallas_call(
        functools.partial(_attn_kernel, kind, geo, li),
        grid=(ns, r // tq),
        in_specs=[pl.BlockSpec((r, w), lambda s, q: (s, 0)),
                  pl.BlockSpec((1, 1, past, ck.shape[-1]), cache_idx),
                  pl.BlockSpec((1, 1, past, cv.shape[-1]), cache_idx),
                  pl.BlockSpec(memory_space=pltpu.SMEM),
                  pl.BlockSpec(lam.shape, lambda s, q: (0, 0)),
                  pl.BlockSpec(gdo.shape, lambda s, q: (0, 0))],
        out_specs=pl.BlockSpec((tq, 256), lambda s, q: (s * (r // tq) + q, 0)),
        out_shape=jax.ShapeDtypeStruct((t_all, 256), F32),
        scratch_shapes=[pltpu.VMEM((r + past, wk), BF16), pltpu.VMEM((r + past, 2 * wk), BF16)],
        compiler_params=_params(("arbitrary", "arbitrary")),
    )(p, ck, cv, sink, lam, gdo)


def _ssd_kernel(geo, li, p_ref, st0_ref, cw_ref, cb_ref, dtb_ref, alog_ref, dsk_ref, gs_ref,
                y_ref, so_ref, xpad, u_scr, dt_scr, la_scr, y_scr, upd_scr, cd_scr, ex_scr, st_scr, sf_ref, sb_ref):
    r, seq, ng = geo["r"], geo["seq"], geo["n_ctx_groups"]
    q = SSM_CHUNK
    nh = SSM_HEADS
    is_ctx = pl.program_id(0) < ng
    inner = nh * SSM_HEAD_DIM
    conv_ch = inner + 2 * LANES
    nchunks = r // q
    spg = r // seq
    gap = 8

    def fill(n_seq, length):
        for j in range(n_seq + 1):
            xpad[j * (length + gap):j * (length + gap) + gap, :] = jnp.zeros((gap, conv_ch), F32)
        for j in range(n_seq):
            xpad[gap + j * (length + gap):gap + j * (length + gap) + length, :] = (
                p_ref[j * length:(j + 1) * length, inner:inner + conv_ch])

    @pl.when(is_ctx)
    def _():
        fill(spg, seq)

    @pl.when(jnp.logical_not(is_ctx))
    def _():
        fill(1, r)

    a_neg = -jnp.exp(alog_ref[...])
    seq_shift = jnp.where(is_ctx, int(math.log2(seq // q)), int(math.log2(r // q)))

    def pre(c, carry):
        r0 = pl.multiple_of(c * q, q)
        st = pl.multiple_of(r0 + lax.shift_right_logical(c, seq_shift) * gap, gap)
        big = xpad[pl.ds(st, q + 16), :]
        acc = jnp.zeros((q, conv_ch), F32) + cb_ref[...]
        for j in range(SSM_CONV):
            sh = pltpu.roll(big, (q + 16) - (gap - SSM_CONV // 2 + j), 0)[0:q]
            acc = acc + sh * cw_ref[j:j + 1, :]
        u_scr[pl.ds(r0, q), :] = _silu(acc)
        raw = p_ref[pl.ds(r0, q), inner + conv_ch:inner + conv_ch + LANES] + dtb_ref[...]
        dt = jnp.maximum(raw, 0.0) + jnp.log1p(jnp.exp(-jnp.abs(raw)))
        dt_scr[pl.ds(r0, q), :] = dt
        la_scr[pl.ds(r0, q), :] = dt * a_neg
        return carry

    lax.fori_loop(0, nchunks, pre, 0)

    row_i = lax.broadcasted_iota(I32, (q, q), 0)
    col_i = lax.broadcasted_iota(I32, (q, q), 1)
    keeps = (col_i <= row_i, col_i >= row_i)
    tris = tuple(jnp.where(k, 1.0, 0.0).astype(BF16) for k in keeps)
    lane128 = lax.broadcasted_iota(I32, (q, LANES), 1)
    lane256 = lax.broadcasted_iota(I32, (q, inner), 1)
    heads256 = [(lane256 >= h * SSM_HEAD_DIM) & (lane256 < (h + 1) * SSM_HEAD_DIM) for h in range(nh)]
    lane8 = lax.broadcasted_iota(I32, (8, inner), 1)
    gmask_t = (lax.broadcasted_iota(I32, (LANES, inner), 0) // SSM_STATE
               == lax.broadcasted_iota(I32, (LANES, inner), 1) // (2 * SSM_HEAD_DIM))

    def phase1(c, carry):
        r0 = pl.multiple_of(c * q, q)
        u = u_scr[pl.ds(r0, q), :]
        xs, bm, cm = u[:, 0:inner], u[:, inner:inner + LANES], u[:, inner + LANES:inner + 2 * LANES]
        bm_b = bm.astype(BF16)
        cb = [_dot_nt(jnp.where((lane128 // SSM_STATE) == g, cm, 0.0).astype(BF16), bm_b) for g in range(2)]
        a1, a2, a3 = _split3(la_scr[pl.ds(r0, q), :])
        cums = [_dot(t, a1) + _dot(t, a2) + _dot(t, a3) for t in tris]
        cum = jnp.where(lane128 < nh, cums[0], cums[1])
        pack = jnp.where(lane128 < 2 * nh, cum, pltpu.roll(dt_scr[pl.ds(r0, q), :], 2 * nh, 1))
        pack_t = pack.T
        bm_t = bm.T
        xstack = jnp.concatenate([jnp.where(heads256[h], xs, 0.0).astype(BF16) for h in range(nh)], axis=0)
        ms, bws = [], []
        for d in range(2):
            edge = q - 1 if d == 0 else 0
            m_h, bw_h, ecol, cdx = [], [], [], jnp.zeros((8, inner), F32)
            for h in range(nh):
                c_ = d * nh + h
                colb = jnp.broadcast_to(cum[:, c_:c_ + 1], (q, q))
                crow = pack_t[c_:c_ + 1, :]
                dtrow = pack_t[2 * nh + c_:2 * nh + c_ + 1, :]
                lm = jnp.exp(jnp.where(keeps[d], colb - crow, -jnp.inf))
                m_h.append((cb[h // 2] * lm * dtrow).astype(BF16))
                tot = crow[:, edge:edge + 1]
                bw_h.append((bm_t * (dtrow * jnp.exp(tot - crow))).astype(BF16))
                ecol.append(jnp.exp(colb))
                cdx = jnp.where((lane8 >= h * SSM_HEAD_DIM) & (lane8 < (h + 1) * SSM_HEAD_DIM), jnp.exp(tot), cdx)
            ms.append(jnp.concatenate(m_h, axis=1))
            bws.append(jnp.concatenate(bw_h, axis=1))
            ex_scr[d, pl.ds(r0, q), :] = jnp.concatenate(
                [jnp.where(lane128 < SSM_HEAD_DIM, ecol[0], ecol[1]),
                 jnp.where(lane128 < SSM_HEAD_DIM, ecol[2], ecol[3])], axis=1)
            cd_scr[d, c] = cdx
        res = _dot(jnp.concatenate(ms + bws, axis=0), xstack)
        y_scr[pl.ds(r0, q), :] = res[0:q] + res[q:2 * q]
        upd_scr[0, c] = jnp.where(gmask_t, res[2 * q:3 * q], 0.0)
        upd_scr[1, c] = jnp.where(gmask_t, res[3 * q:4 * q], 0.0)
        return carry

    lax.fori_loop(0, nchunks, phase1, 0, unroll=2)

    srefs = (sf_ref, sb_ref)

    def scan_step(d, c):
        s_in = srefs[d][...]
        st_scr[d, c] = s_in.astype(BF16)
        srefs[d][...] = s_in * cd_scr[d, c][0:1, :] + upd_scr[d, c]

    @pl.when(is_ctx)
    def _ctx():
        cps = seq // q

        def one_seq(j, carry):
            for d in range(2):
                srefs[d][...] = jnp.zeros((LANES, inner), F32)
                for ci in range(cps):
                    scan_step(d, j * cps + (ci if d == 0 else cps - 1 - ci))
                sv = srefs[d][...]
                so_ref[j, d, 0:LANES, :] = sv[:, 0:LANES].T[:, 0:SSM_STATE]
                so_ref[j, d, LANES:inner, :] = pltpu.roll(sv[:, LANES:inner].T, SSM_STATE, 1)[:, 0:SSM_STATE]
            return carry

        lax.fori_loop(0, spg, one_seq, 0)

    @pl.when(jnp.logical_not(is_ctx))
    def _lat():
        for d in range(2):
            srefs[d][...] = st0_ref[0, 0, d]

        def body(i, carry):
            scan_step(0, i)
            scan_step(1, nchunks - 1 - i)
            return carry

        lax.fori_loop(0, nchunks, body, 0)

    def fin(c, carry):
        r0 = pl.multiple_of(c * q, q)
        cm_b = u_scr[pl.ds(r0, q), inner + LANES:inner + 2 * LANES].astype(BF16)
        y = y_scr[pl.ds(r0, q), :] + u_scr[pl.ds(r0, q), 0:inner] * dsk_ref[...]
        for d in range(2):
            y = y + _dot(cm_b, st_scr[d, c]) * ex_scr[d, pl.ds(r0, q), :]
        y = y * _silu(p_ref[pl.ds(r0, q), 0:inner])
        y_ref[pl.ds(r0, q), :] = _rms(y, gs_ref[...])
        return carry

    lax.fori_loop(0, nchunks, fin, 0, unroll=4)


def _ssd(geo, li, pc, st0, cw, cb, dtb, alog, dsk, gs):
    t_all, w = pc.shape
    r, seq, ng = geo["r"], geo["seq"], geo["n_ctx_groups"]
    ns = t_all // r
    spg = r // seq
    nchunks = r // SSM_CHUNK
    inner = SSM_HEADS * SSM_HEAD_DIM
    full = lambda a: pl.BlockSpec(a.shape, lambda s: (0,) * a.ndim)
    return pl.pallas_call(
        functools.partial(_ssd_kernel, geo, li),
        grid=(ns,),
        in_specs=[pl.BlockSpec((r, w), lambda s: (s, 0)),
                  pl.BlockSpec((1, 1, 2, LANES, inner), lambda s: (jnp.maximum(s - ng, 0), li, 0, 0, 0)),
                  full(cw), full(cb), full(dtb), full(alog), full(dsk), full(gs)],
        out_specs=[pl.BlockSpec((r, inner), lambda s: (s, 0)),
                   pl.BlockSpec((spg, 2, inner, SSM_STATE), lambda s: (jnp.minimum(s, ng - 1), 0, 0, 0))],
        out_shape=[jax.ShapeDtypeStruct((t_all, inner), F32),
                   jax.ShapeDtypeStruct((ng * spg, 2, inner, SSM_STATE), F32)],
        scratch_shapes=[pltpu.VMEM((r + 8 * (spg + 1), 2 * inner), F32), pltpu.VMEM((r, 2 * inner), F32),
                        pltpu.VMEM((r, LANES), F32), pltpu.VMEM((r, LANES), F32),
                        pltpu.VMEM((r, inner), F32),
                        pltpu.VMEM((2, nchunks, LANES, inner), F32), pltpu.VMEM((2, nchunks, 8, inner), F32),
                        pltpu.VMEM((2, r, inner), F32), pltpu.VMEM((2, nchunks, LANES, inner), BF16),
                        pltpu.VMEM((LANES, inner), F32), pltpu.VMEM((LANES, inner), F32)],
        compiler_params=_params(("arbitrary",)),
    )(pc, st0, cw, cb, dtb, alog, dsk, gs)


def _merge_kernel(nct, xc_ref, xl_ref, mod_ref, g1_ref, g2_ref, ya_ref, yb_ref, yc_ref, yd_ref, wg_ref, wbr_ref,
                  wo_ref, wr_ref, x1_ref, h2_ref, aff_ref):
    tm, d = xc_ref.shape
    is_ctx = pl.program_id(0) < nct
    m = mod_ref[0]
    w_hi, w_lo = _split2(wr_ref[...])
    n_sub = 2
    ts = tm // n_sub

    def mix(i):
        rows = slice(i * ts, (i + 1) * ts)
        x = jnp.where(is_ctx, xc_ref[rows, :], xl_ref[rows, :])
        hb = (_rms(x, g1_ref[...]) * (1.0 + m[1:2]) + m[0:1]).astype(BF16)
        merged = jnp.zeros(x.shape, F32)
        for n, y_ref in enumerate((ya_ref, yb_ref, yc_ref, yd_ref)):
            gate = jax.nn.sigmoid(_dot(hb, wg_ref[:, n * d:(n + 1) * d]))
            merged = merged + gate * _dot(y_ref[rows, :].astype(BF16), wbr_ref[n])
        return x + m[2:3] * _dot(merged.astype(BF16), wo_ref[...])

    def tail(i, x1):
        rows = slice(i * ts, (i + 1) * ts)
        x1_ref[rows, :] = x1
        h2 = _rms(x1, g2_ref[...]) * (1.0 + m[4:5]) + m[3:4]
        h2_ref[rows, :] = h2.astype(BF16)
        h_hi, h_lo = _split2(h2)
        logit = _dot_nt(w_hi, h_hi) + _dot_nt(w_hi, h_lo) + _dot_nt(w_lo, h_hi)
        e = jnp.exp(logit - logit.max(axis=0, keepdims=True))
        aff_ref[:, rows] = e / e.sum(axis=0, keepdims=True)

    x1_prev = mix(0)
    for i in range(1, n_sub):
        x1_next = mix(i)
        tail(i - 1, x1_prev)
        x1_prev = x1_next
    tail(n_sub - 1, x1_prev)


def _merge(x, mod, g1, g2, ys, wg, wbr, wo, wr_t, geo, tm):
    d = x[0].shape[1]
    t_all = x[0].shape[0] + x[1].shape[0]
    ne = wr_t.shape[0]
    nct = geo["t_ctx"] // tm
    tps = geo["r"] // tm
    mod_row = lambda i: jnp.where(i < nct, 0, 1 + (i - nct) // tps)
    full = lambda a: pl.BlockSpec(a.shape, lambda i: (0,) * a.ndim, pipeline_mode=pl.Buffered(1))
    ytile = pl.BlockSpec((tm, BRANCH_W), lambda i: (i, 0))
    return pl.pallas_call(
        functools.partial(_merge_kernel, nct),
        grid=(t_all // tm,),
        in_specs=_x_specs(nct, tm, d) + [
                  pl.BlockSpec((1, 6, d), lambda i: (mod_row(i), 0, 0)),
                  full(g1), full(g2), ytile, ytile, ytile, ytile, full(wg), full(wbr), full(wo), full(wr_t)],
        out_specs=[pl.BlockSpec((tm, d), lambda i: (i, 0)),
                   pl.BlockSpec((tm, d), lambda i: (i, 0)),
                   pl.BlockSpec((ne, tm), lambda i: (0, i))],
        out_shape=[jax.ShapeDtypeStruct((t_all, d), F32),
                   jax.ShapeDtypeStruct((t_all, d), BF16),
                   jax.ShapeDtypeStruct((ne, t_all), F32)],
        compiler_params=_params(("arbitrary",)),
    )(x[0], x[1], mod, g1, g2, *ys, wg, wbr, wo, wr_t)


def _kth_largest_bits(a, cap):
    def body(i, thr):
        cand = thr | lax.shift_left(jnp.int32(1), 30 - i)
        cnt = jnp.sum(jnp.where(a >= lax.bitcast_convert_type(cand, F32), 1.0, 0.0), axis=1, keepdims=True)
        return jnp.where(cnt >= cap, cand, thr)
    return lax.fori_loop(0, 31, body, jnp.zeros((a.shape[0], 1), I32))


def _route_kernel(geo, aff_ref, slot_ref, start_ref):
    r, seq, ng = geo["r"], geo["seq"], geo["n_ctx_groups"]
    ne = aff_ref.shape[0]
    blk = ROUTE_TILE
    s = pl.program_id(0)
    lane_s = lax.broadcasted_iota(I32, (ne, LANES), 1)
    upper = jnp.where(lax.broadcasted_iota(I32, (blk, blk), 0) < lax.broadcasted_iota(I32, (blk, blk), 1),
                      1.0, 0.0).astype(BF16)

    def select(a, cap, nblk):
        thr = _kth_largest_bits(a, cap)
        gt = a >= lax.bitcast_convert_type(thr + 1, F32)
        eq = (a >= lax.bitcast_convert_type(thr, F32)) & jnp.logical_not(gt)
        need = cap - jnp.sum(jnp.where(gt, 1.0, 0.0), axis=1, keepdims=True)
        eq_f = jnp.where(eq, 1.0, 0.0)
        outs, firsts = [], []
        carry_e = jnp.zeros((a.shape[0], 1), F32)
        carry_s = jnp.zeros((a.shape[0], 1), F32)
        for b in range(nblk):
            sl = slice(b * blk, (b + 1) * blk)
            pe = _dot(eq_f[:, sl].astype(BF16), upper) + carry_e
            sel = gt[:, sl] | (eq[:, sl] & (pe < need))
            sel_f = jnp.where(sel, 1.0, 0.0)
            ps = _dot(sel_f.astype(BF16), upper) + carry_s
            outs.append(jnp.where(sel, ps, -1.0))
            firsts.append(carry_s)
            carry_e = carry_e + eq_f[:, sl].sum(axis=1, keepdims=True)
            carry_s = carry_s + sel_f.sum(axis=1, keepdims=True)
        return outs, firsts

    @pl.when(s < ng)
    def _ctx():
        spg = r // seq
        cap = EC_FACTOR * seq // ne
        a = jnp.concatenate([aff_ref[:, j * seq:(j + 1) * seq] for j in range(spg)], axis=0)
        out, _ = select(a, cap, 1)
        for j in range(spg):
            o = out[0][j * ne:(j + 1) * ne]
            slot_ref[:, j * seq:(j + 1) * seq] = jnp.where(o >= 0, o + j * cap, -1.0).astype(I32)
        start_ref[...] = jnp.minimum(lane_s, spg) * cap

    @pl.when(s >= ng)
    def _lat():
        cap = EC_FACTOR * r // ne
        out, firsts = select(aff_ref[...], cap, r // blk)
        st = jnp.full((ne, LANES), float(cap), F32)
        for b, o in enumerate(out):
            slot_ref[:, b * blk:(b + 1) * blk] = o.astype(I32)
            st = jnp.where(lane_s == b, firsts[b], st)
        start_ref[...] = st.astype(I32)


def _route(aff_t, geo):
    ne, t_all = aff_t.shape
    r = geo["r"]
    return pl.pallas_call(
        functools.partial(_route_kernel, geo),
        grid=(t_all // r,),
        in_specs=[pl.BlockSpec((ne, r), lambda s: (0, s))],
        out_specs=[pl.BlockSpec((ne, r), lambda s: (0, s)), pl.BlockSpec((ne, LANES), lambda s: (0, s))],
        out_shape=[jax.ShapeDtypeStruct((ne, t_all), I32), jax.ShapeDtypeStruct((ne, (t_all // r) * LANES), I32)],
        compiler_params=_params(("arbitrary",)),
    )(aff_t)


def _tile_starts(starts_ref, e, s, ns, nt):
    base = (e * ns + s) * (nt + 1)
    return [starts_ref[base + i] for i in range(nt + 1)]


def _expert_kernel(starts_ref, x_ref, slot_ref, aff_ref, w1_ref, w3_ref, w2_ref, yo_ref, wb1, wb3, wb2,
                   xs_scr, gs_scr):
    e, s, ns = pl.program_id(0), pl.program_id(1), pl.num_programs(1)

    @pl.when(s == 0)
    def _cast():
        wb1[...] = w1_ref[0, 0].astype(BF16)
        wb3[...] = w3_ref[0, 0].astype(BF16)
        wb2[...] = w2_ref[0, 0].astype(BF16)

    cap = yo_ref.shape[2]
    r = x_ref.shape[0]
    nt = r // ROUTE_TILE
    win = min(GATHER_WINDOW, cap)
    slot = slot_ref[0]
    aff = aff_ref[0]
    a = _tile_starts(starts_ref, e, s, ns, nt)
    fits = a[1] - a[0] <= win - 16
    for i in range(1, nt):
        fits = fits & (a[i + 1] - a[i] <= win - 16)

    @pl.when(fits)
    def _windows():
        xs_scr[...] = jnp.zeros(xs_scr.shape, F32)
        gs_scr[...] = jnp.zeros(gs_scr.shape, F32)
        for i in range(nt):
            cols = slice(i * ROUTE_TILE, (i + 1) * ROUTE_TILE)
            w0 = pl.multiple_of(jnp.minimum((a[i] // 16) * 16, cap - win), 16)
            pick = (lax.broadcasted_iota(I32, (win, ROUTE_TILE), 0) + w0) == slot[:, cols]
            xs_scr[pl.ds(w0, win), :] += _dot(jnp.where(pick, 1.0, 0.0).astype(BF16), x_ref[cols, :])
            gs_scr[pl.ds(w0, win), :] += jnp.sum(jnp.where(pick, aff[:, cols], 0.0), axis=1, keepdims=True)

    @pl.when(jnp.logical_not(fits))
    def _all_rows():
        pick = lax.broadcasted_iota(I32, (cap, r), 0) == slot
        xs_scr[...] = _dot(jnp.where(pick, 1.0, 0.0).astype(BF16), x_ref[...])
        gs_scr[...] = jnp.broadcast_to(jnp.sum(jnp.where(pick, aff, 0.0), axis=1, keepdims=True), gs_scr.shape)

    xs = xs_scr[...].astype(BF16)
    hid = _silu(_dot(xs, wb1[...])) * _dot(xs, wb3[...])
    yo_ref[0, 0] = (_dot(hid.astype(BF16), wb2[...]) * gs_scr[:, 0:1]).astype(BF16)


def _experts(starts, h2, slot3, aff3, w1, w3, w2, geo, li):
    t_all, d = h2.shape
    _, ne, _, f = w1.shape
    r = geo["r"]
    ns = t_all // r
    cap = EC_FACTOR * r // ne
    wspec = lambda a: pl.BlockSpec((1, 1) + a.shape[2:], lambda e, s, st: (li, e, 0, 0))
    return pl.pallas_call(
        _expert_kernel,
        grid_spec=pltpu.PrefetchScalarGridSpec(
            num_scalar_prefetch=1,
            grid=(ne, ns),
            in_specs=[pl.BlockSpec((r, d), lambda e, s, st: (s, 0)),
                      pl.BlockSpec((1, 1, r), lambda e, s, st: (e, 0, s)),
                      pl.BlockSpec((1, 1, r), lambda e, s, st: (e, 0, s)),
                      wspec(w1), wspec(w3), wspec(w2)],
            out_specs=pl.BlockSpec((1, 1, cap, d), lambda e, s, st: (s, e, 0, 0)),
            scratch_shapes=[pltpu.VMEM((d, f), BF16), pltpu.VMEM((d, f), BF16), pltpu.VMEM((f, d), BF16),
                            pltpu.VMEM((cap, d), F32), pltpu.VMEM((cap, LANES), F32)]),
        out_shape=jax.ShapeDtypeStruct((ns, ne, cap, d), BF16),
        compiler_params=_params(("arbitrary", "arbitrary")),
    )(starts, h2, slot3, aff3, w1, w3, w2)


def _scatter_kernel(ng, starts_ref, x_ref, mod_ref, slot_ref, yo_ref, oc_ref, ol_ref, ywin, ffn_scr):
    tt = x_ref.shape[0]
    _, ne, cap, d = yo_ref.shape
    s, i, ns, nt = pl.program_id(0), pl.program_id(1), pl.num_programs(0), pl.num_programs(1)
    win = min(SCATTER_WINDOW, cap)
    slot = slot_ref[...]
    first = [starts_ref[(e * ns + s) * (nt + 1) + i] for e in range(ne)]
    last = [starts_ref[(e * ns + s) * (nt + 1) + i + 1] for e in range(ne)]
    fits = last[0] - first[0] <= win - 16
    for e in range(1, ne):
        fits = fits & (last[e] - first[e] <= win - 16)

    @pl.when(fits)
    def _windows():
        lane = lax.broadcasted_iota(I32, (tt, win), 1)
        hot = []
        for e in range(ne):
            w0 = pl.multiple_of(jnp.minimum((first[e] // 16) * 16, cap - win), 16)
            ywin[e * win:(e + 1) * win, :] = yo_ref[0, e, pl.ds(w0, win), :]
            hot.append(jnp.where(slot[:, e:e + 1] - w0 == lane, 1.0, 0.0).astype(BF16))
        ffn_scr[...] = _dot(jnp.concatenate(hot, axis=1), ywin[...])

    @pl.when(jnp.logical_not(fits))
    def _all_rows():
        lane = lax.broadcasted_iota(I32, (tt, cap), 1)
        hot = jnp.concatenate([jnp.where(slot[:, e:e + 1] == lane, 1.0, 0.0).astype(BF16) for e in range(ne)], axis=1)
        ffn_scr[...] = _dot(hot, yo_ref[0].reshape(ne * cap, d))

    out = x_ref[...] + mod_ref[0][5:6] * ffn_scr[...]
    is_ctx = pl.program_id(0) < ng

    @pl.when(is_ctx)
    def _():
        oc_ref[...] = out

    @pl.when(jnp.logical_not(is_ctx))
    def _():
        ol_ref[...] = out


def _scatter(starts, x1, mod, slot_t, yo, geo):
    t_all, d = x1.shape
    ns, ne, cap, _ = yo.shape
    r = geo["r"]
    ng = geo["n_ctx_groups"]
    tt = ROUTE_TILE
    tpg = r // tt
    nct = ng * tpg
    win = min(SCATTER_WINDOW, cap)
    mod_row = lambda s, i, st: jnp.where(s < ng, 0, 1 + s - ng)
    out_specs = [pl.BlockSpec((tt, d), lambda s, i, st: (jnp.minimum(s * tpg + i, nct - 1), 0)),
                 pl.BlockSpec((tt, d), lambda s, i, st: (jnp.maximum(s * tpg + i - nct, 0), 0))]
    out_shape = [jax.ShapeDtypeStruct((nct * tt, d), F32), jax.ShapeDtypeStruct((t_all - nct * tt, d), F32)]
    return pl.pallas_call(
        functools.partial(_scatter_kernel, ng),
        grid_spec=pltpu.PrefetchScalarGridSpec(
            num_scalar_prefetch=1,
            grid=(ns, tpg),
            in_specs=[pl.BlockSpec((tt, d), lambda s, i, st: (s * tpg + i, 0)),
                      pl.BlockSpec((1, 6, d), lambda s, i, st: (mod_row(s, i, st), 0, 0)),
                      pl.BlockSpec((tt, ne), lambda s, i, st: (s * tpg + i, 0)),
                      pl.BlockSpec((1, ne, cap, d), lambda s, i, st: (s, 0, 0, 0))],
            out_specs=out_specs,
            scratch_shapes=[pltpu.VMEM((ne * win, d), BF16), pltpu.VMEM((tt, d), F32)]),
        out_shape=out_shape,
        compiler_params=_params(("arbitrary", "arbitrary")),
    )(starts, x1, mod, slot_t, yo)


def _rope_table(s, d, n_qk_lanes, n_v_lanes, pad_rows):
    n = d // 4
    t = np.arange(s)
    rows = (t // GRID_W).astype(np.float64)
    cols = (t % GRID_W).astype(np.float64)
    inv = ROPE_THETA ** (-np.arange(n, dtype=np.float64) / n)
    ang = np.stack([rows[:, None] * inv, cols[:, None] * inv], axis=1)
    cos, sin = np.cos(ang), np.sin(ang)
    cos_h = np.concatenate([cos[:, 0], cos[:, 0], cos[:, 1], cos[:, 1]], axis=-1)
    sin_h = np.concatenate([-sin[:, 0], sin[:, 0], -sin[:, 1], sin[:, 1]], axis=-1)
    reps = n_qk_lanes // d
    cos_f = np.concatenate([np.tile(cos_h, (1, reps)), np.ones((s, n_v_lanes))], axis=1)
    sin_f = np.concatenate([np.tile(sin_h, (1, reps)), np.zeros((s, n_v_lanes))], axis=1)
    w = n_qk_lanes + n_v_lanes
    cos_f = np.concatenate([cos_f, np.ones((pad_rows, w))], axis=0)
    sin_f = np.concatenate([sin_f, np.zeros((pad_rows, w))], axis=0)
    return jnp.asarray(cos_f.astype(np.float32)), jnp.asarray(sin_f.astype(np.float32))


def _block_diag_ones(w, gsize):
    i = np.arange(w)
    return jnp.asarray((i[:, None] // gsize == i[None, :] // gsize).astype(np.float32), dtype=BF16)


def _pad_lanes(a, w):
    return jnp.pad(a, ((0, 0), (0, w - a.shape[1])))


def kernel(x_prompt, x_sample, c, cache_a_k, cache_a_v, cache_b_k, cache_b_v, state_ssm, cache_d_k, cache_d_v, c_ctx, w_ada, b_ada, g_norm1, g_norm2, w_in, g_qa, g_ka, g_qb, g_kb, sink_b, conv_w, conv_b, dt_bias, a_log, d_skip, g_ssm, g_qd, g_kd, lam_q1, lam_k1, lam_q2, lam_k2, g_dout, w_br, w_out, w_router, w_e1, w_e3, w_e2):
    batch, seq, d = x_prompt.shape
    dec_batch, dec_seq, _ = x_sample.shape
    depth = w_in.shape[0]
    past = cache_a_k.shape[2]
    ne = w_router.shape[2]
    r = dec_seq
    t_ctx = batch * seq
    assert t_ctx % r == 0 and r % seq == 0 and seq == 256 and past % 16 == 0
    tm = 512
    assert t_ctx % tm == 0 and r % tm == 0
    geo = dict(r=r, seq=seq, past=past, tq=256, t_ctx=t_ctx, n_ctx_groups=t_ctx // r)
    ng = geo["n_ctx_groups"]

    cosa, sina = _rope_table(r, HEAD_DIM, 384, 128, tm)
    cosd, sind = _rope_table(r, DIFF_QK_DIM, 512, 256, tm)
    bda = _block_diag_ones(256, HEAD_DIM)
    bdd = _block_diag_ones(256, DIFF_QK_DIM)
    ones = lambda n: jnp.ones((n,), F32)
    inner = SSM_HEADS * SSM_HEAD_DIM
    gsel = (np.arange(LANES)[:, None] // SSM_STATE) == (np.arange(inner)[None, :] // (2 * SSM_HEAD_DIM))

    nm = 16
    cvec = jnp.concatenate([c_ctx[None, :], c, jnp.zeros((nm - 1 - dec_batch, d), F32)], axis=0)
    mod_all = _adaln(cvec, w_ada, b_ada)

    x = (x_prompt.reshape(t_ctx, d), x_sample.reshape(dec_batch * r, d))
    ck_a = cache_a_k.reshape(dec_batch, depth, past, -1)
    cv_a = cache_a_v.reshape(dec_batch, depth, past, -1)
    ck_b = cache_b_k.reshape(dec_batch, depth, past, -1)
    cv_b = cache_b_v.reshape(dec_batch, depth, past, -1)
    ck_d = cache_d_k.reshape(dec_batch, depth, past, -1)
    cv_d = cache_d_v.reshape(dec_batch, depth, past, -1)
    st0 = jnp.moveaxis(state_ssm, -1, -3).reshape(dec_batch, depth, 2, SSM_STATE, inner)
    st0 = jnp.where(gsel, jnp.concatenate([st0, st0], axis=-2), 0.0)

    new = {k: [] for k in ("a_k", "a_v", "b_k", "b_v", "ssm", "d_k", "d_v")}
    for li in range(depth):
        mod = mod_all[li].reshape(nm, 6, d)
        wl = w_in[li]
        w_mix = jnp.concatenate([wl[:, :DT_SRC], jnp.zeros((d, COL_C[1] - COL_C[0] - 776), F32),
                                 wl[:, DT_SRC:N_MIX_SRC]], axis=1).astype(BF16)
        w_gate = wl[:, N_MIX_SRC:].astype(BF16)
        consts = dict(
            bda=bda, bdd=bdd, cosa=cosa, sina=sina, cosd=cosd, sind=sind,
            ga=jnp.concatenate([jnp.tile(g_qa[li], 4), jnp.tile(g_ka[li], 2), ones(128)])[None, :],
            gb=jnp.concatenate([jnp.tile(g_qb[li], 4), jnp.tile(g_kb[li], 2), ones(128)])[None, :],
            gd=jnp.concatenate([jnp.tile(g_qd[li], 8), jnp.tile(g_kd[li], 8), ones(256)])[None, :])
        g1 = g_norm1[li][None, :]
        g2 = g_norm2[li][None, :]
        pa, pb, pc, pd = _inproj(x, mod, g1, w_mix, consts, geo, tm)

        lam = jnp.stack([lam_q1[li], lam_k1[li], lam_q2[li], lam_k2[li]], axis=0)
        gdo = jnp.tile(g_dout[li], 4)[None, :]
        ya = _attention("A", geo, li, pa, ck_a, cv_a, sink_b, lam, gdo)
        yb = _attention("B", geo, li, pb, ck_b, cv_b, sink_b, lam, gdo)
        yd = _attention("D", geo, li, pd, ck_d, cv_d, sink_b, lam, gdo)
        yc, ssm_new = _ssd(geo, li, pc, st0,
                           jnp.pad(conv_w[li], ((0, 8 - SSM_CONV), (0, 0))), conv_b[li][None, :],
                           _pad_lanes(dt_bias[li].reshape(1, -1), LANES), _pad_lanes(a_log[li].reshape(1, -1), LANES),
                           jnp.repeat(d_skip[li], SSM_HEAD_DIM)[None, :], g_ssm[li][None, :])

        x1, h2, aff_t = _merge(x, mod, g1, g2, (ya, yb, yc, yd), w_gate, w_br[li].astype(BF16),
                               w_out[li].astype(BF16), w_router[li].T, geo, tm)
        slot, starts = _route(aff_t, geo)
        starts = starts.reshape(ne, -1, LANES)[:, :, :r // ROUTE_TILE + 1].reshape(-1)
        yo = _experts(starts, h2, slot.reshape(ne, 1, -1), aff_t.reshape(ne, 1, -1), w_e1, w_e3, w_e2, geo, li)
        x = _scatter(starts, x1, mod, slot.T, yo, geo)

        ctx = lambda p, lo, hi: p[:t_ctx, lo:hi]
        new["a_k"].append(ctx(pa, 256, 384).reshape(batch, seq, 2, HEAD_DIM))
        new["a_v"].append(ctx(pa, 384, 512).reshape(batch, seq, 2, HEAD_DIM))
        new["b_k"].append(ctx(pb, 256, 384).reshape(batch, seq, 2, HEAD_DIM))
        new["b_v"].append(ctx(pb, 384, 512).reshape(batch, seq, 2, HEAD_DIM))
        new["d_k"].append(ctx(pd, 256, 512).reshape(batch, seq, 4, 2, DIFF_QK_DIM))
        new["d_v"].append(ctx(pd, 512, 768).reshape(batch, seq, 4, HEAD_DIM))
        new["ssm"].append(ssm_new.reshape(batch, 2, SSM_HEADS, SSM_HEAD_DIM, SSM_STATE))

    st = lambda k: jnp.stack(new[k], axis=1)
    return (x[0].reshape(batch, seq, d), x[1].reshape(dec_batch, r, d),
            st("a_k"), st("a_v"), st("b_k"), st("b_v"), st("ssm"), st("d_k"), st("d_v"))
```

```python
import functools
import math

import numpy as np
import jax
import jax.numpy as jnp
from jax import lax
from jax.experimental import pallas as pl
from jax.experimental.pallas import tpu as pltpu

F32 = jnp.float32
BF16 = jnp.bfloat16
I32 = jnp.int32

HEAD_DIM = 64
DIFF_QK_DIM = 32
GRID_W = 64
ROPE_THETA = 10000.0
NORM_EPS = 1e-6
WINDOW = 128
SSM_CHUNK = 128
SSM_HEADS = 4
SSM_HEAD_DIM = 64
SSM_STATE = 64
SSM_CONV = 5
EC_FACTOR = 2
BRANCH_W = 256
N_BRANCH = 4

ROUTE_TILE = 256
GATHER_WINDOW = 80
SCATTER_WINDOW = 128

MOD_ROWS = 16
LANES = 128
VMEM_LIMIT = 56 * 1024 * 1024

COL_A = (0, 512)
COL_B = (512, 1024)
COL_C = (1024, 1920)
COL_D = (1920, 2688)
W_MIX = 2688
N_MIX_SRC = 2568
DT_SRC = 1800


def _dot(a, b):
    return jnp.dot(a, b, preferred_element_type=F32)


def _dot_nt(a, b):
    return lax.dot_general(a, b, (((1,), (1,)), ((), ())), preferred_element_type=F32)


def _dot_tn(a, b):
    return lax.dot_general(a, b, (((0,), (0,)), ((), ())), preferred_element_type=F32)


def _split2(x):
    hi = x.astype(BF16)
    lo = (x - hi.astype(F32)).astype(BF16)
    return hi, lo


def _split3(x):
    hi = x.astype(BF16)
    r = x - hi.astype(F32)
    mid = r.astype(BF16)
    lo = (r - mid.astype(F32)).astype(BF16)
    return hi, mid, lo


def _silu(x):
    return x * jax.nn.sigmoid(x)


def _rms(x, g):
    ms = jnp.mean(x * x, axis=-1, keepdims=True)
    return x * lax.rsqrt(ms + NORM_EPS) * g


def _params(sem):
    return pltpu.CompilerParams(dimension_semantics=sem, vmem_limit_bytes=VMEM_LIMIT)


def _adaln_kernel(c_ref, w_ref, b_ref, o_ref):
    s_hi, s_lo = _split2(_silu(c_ref[...]))
    w_hi, w_lo = _split2(w_ref[0])
    o_ref[0] = _dot(s_hi, w_hi) + _dot(s_hi, w_lo) + _dot(s_lo, w_hi) + b_ref[0]


def _adaln(cvec, w_ada, b_ada):
    depth, d, n = w_ada.shape
    m = cvec.shape[0]
    tn = 1024
    return pl.pallas_call(
        _adaln_kernel,
        grid=(depth, n // tn),
        in_specs=[pl.BlockSpec((m, d), lambda l, j: (0, 0)),
                  pl.BlockSpec((1, d, tn), lambda l, j: (l, 0, j)),
                  pl.BlockSpec((1, 1, tn), lambda l, j: (l, 0, j))],
        out_specs=pl.BlockSpec((1, m, tn), lambda l, j: (l, 0, j)),
        out_shape=jax.ShapeDtypeStruct((depth, m, n), F32),
        compiler_params=_params(("arbitrary", "arbitrary")),
    )(cvec, w_ada, b_ada.reshape(depth, 1, n))


def _qknorm_rope_store(p, bd, gain, cos_ref, sin_ref, segs, gsize, n, o_ref):
    parts = []
    done = 0
    for (lo, hi) in segs:
        ps = p[:, lo:hi]
        ssum = _dot((ps * ps).astype(BF16), bd[0:hi - lo, 0:hi - lo])
        parts.append(ps * lax.rsqrt(ssum * (1.0 / gsize) + NORM_EPS) * gain[:, lo:hi])
        done = hi
    parts.append(p[:, done:])
    y = jnp.concatenate(parts, axis=1)
    for c in range(p.shape[1] // LANES):
        sl = slice(c * LANES, (c + 1) * LANES)
        ys = y[:, sl]
        up = pltpu.roll(ys, LANES - n, 1)
        dn = pltpu.roll(ys, n, 1)
        l128 = lax.broadcasted_iota(I32, ys.shape, 1)
        sw = jnp.where((l128 & (2 * n - 1)) < n, up, dn)
        o_ref[:, sl] = ys * cos_ref[:, sl] + sw * sin_ref[:, sl]


def _pick_x(nct, xc_ref, xl_ref):
    return jnp.where(pl.program_id(0) < nct, xc_ref[...], xl_ref[...])


def _x_specs(nct, tm, d):
    return [pl.BlockSpec((tm, d), lambda i: (jnp.minimum(i, nct - 1), 0)),
            pl.BlockSpec((tm, d), lambda i: (jnp.maximum(i - nct, 0), 0))]


def _inproj_kernel(nct, xc_ref, xl_ref, mod_ref, g1_ref, w_ref, bda_ref, bdd_ref, ga_ref, gb_ref, gd_ref,
                   cosa_ref, sina_ref, cosd_ref, sind_ref, pa_ref, pb_ref, pc_ref, pd_ref):
    g1_ref, w_ref, ga_ref, gb_ref, gd_ref = (t.at[0] for t in (g1_ref, w_ref, ga_ref, gb_ref, gd_ref))
    m = mod_ref[0]
    h = _rms(_pick_x(nct, xc_ref, xl_ref), g1_ref[...]) * (1.0 + m[1:2]) + m[0:1]
    hb = h.astype(BF16)
    pa = _dot(hb, w_ref[:, COL_A[0]:COL_A[1]])
    pb = _dot(hb, w_ref[:, COL_B[0]:COL_B[1]])
    _qknorm_rope_store(pa, bda_ref[...], ga_ref[...], cosa_ref, sina_ref, ((0, 256), (256, 384)), HEAD_DIM, HEAD_DIM // 4, pa_ref)
    pd = _dot(hb, w_ref[:, COL_D[0]:COL_D[1]])
    _qknorm_rope_store(pb, bda_ref[...], gb_ref[...], cosa_ref, sina_ref, ((0, 256), (256, 384)), HEAD_DIM, HEAD_DIM // 4, pb_ref)
    pc_ref[...] = _dot(hb, w_ref[:, COL_C[0]:COL_C[1]])
    _qknorm_rope_store(pd, bdd_ref[...], gd_ref[...], cosd_ref, sind_ref, ((0, 256), (256, 512)), DIFF_QK_DIM, DIFF_QK_DIM // 4, pd_ref)


def _layer_spec(a, li, n_grid):
    zeros = (0,) * (a.ndim - 1)
    if n_grid == 1:
        return pl.BlockSpec((1,) + a.shape[1:], lambda i: (li,) + zeros, pipeline_mode=pl.Buffered(1))
    return pl.BlockSpec((1,) + a.shape[1:], lambda i, j: (li,) + zeros, pipeline_mode=pl.Buffered(1))


def _inproj(x, mod, g1, w_mix, consts, geo, tm, li):
    d = x[0].shape[1]
    t_all = x[0].shape[0] + x[1].shape[0]
    nct = geo["t_ctx"] // tm
    tps = geo["r"] // tm
    nb = x[1].shape[0] // geo["r"]
    seq_of = lambda i: (i - nct) % nb
    pos_of = lambda i: (i - nct) // nb
    lat_tile = lambda i: seq_of(i) * tps + pos_of(i)

    def mod_row(i):
        return li * MOD_ROWS + jnp.where(i < nct, 0, 1 + seq_of(i))

    def rope_row(i):
        return jnp.where(i < nct, tps, pos_of(i))

    full = lambda a: pl.BlockSpec(a.shape, lambda i: (0,) * a.ndim, pipeline_mode=pl.Buffered(1))
    layer = lambda a: _layer_spec(a, li, 1)
    widths = [COL_A[1] - COL_A[0], COL_B[1] - COL_B[0], COL_C[1] - COL_C[0], COL_D[1] - COL_D[0]]
    tab = lambda w: pl.BlockSpec((tm, w), lambda i: (rope_row(i), 0))
    return pl.pallas_call(
        functools.partial(_inproj_kernel, nct),
        grid=(t_all // tm,),
        in_specs=[pl.BlockSpec((tm, d), lambda i: (jnp.minimum(i, nct - 1), 0)),
                  pl.BlockSpec((tm, d), lambda i: (jnp.where(i < nct, 0, lat_tile(i)), 0)),
                  pl.BlockSpec((1, 6, d), lambda i: (mod_row(i), 0, 0)),
                  layer(g1), layer(w_mix), full(consts["bda"]), full(consts["bdd"]),
                  layer(consts["ga"]), layer(consts["gb"]), layer(consts["gd"]),
                  tab(512), tab(512), tab(768), tab(768)],
        out_specs=[pl.BlockSpec((tm, w), lambda i: (jnp.where(i < nct, i, nct + lat_tile(i)), 0)) for w in widths],
        out_shape=[jax.ShapeDtypeStruct((t_all, w), F32) for w in widths],
        compiler_params=_params(("arbitrary",)),
    )(x[0], x[1], mod, g1, w_mix, consts["bda"], consts["bdd"], consts["ga"], consts["gb"], consts["gd"],
      consts["cosa"], consts["sina"], consts["cosd"], consts["sind"])


def _values_with_ones(v):
    lane = lax.broadcasted_iota(I32, (v.shape[0], LANES), 1)
    slabs = []
    for c in range(v.shape[1] // LANES):
        vs = v[:, c * LANES:(c + 1) * LANES]
        slabs.append(jnp.where(lane < HEAD_DIM, vs, 1.0))
        slabs.append(jnp.where(lane < HEAD_DIM, pltpu.roll(vs, HEAD_DIM, 1), 1.0))
    return jnp.concatenate(slabs, axis=1).astype(BF16)


def _attn_kernel(kind, geo, li, p_ref, ck_ref, cv_ref, sink_ref, lam_ref, gdo_ref, y_ref, kb_ref, vb_ref):
    r, past, tq, ng = geo["r"], geo["past"], geo["tq"], geo["n_ctx_groups"]
    lam_ref, gdo_ref = lam_ref.at[0], gdo_ref.at[0]
    s = pl.program_id(0)
    qi = pl.program_id(1)
    is_ctx = s < ng
    diff = kind == "D"
    kcols = (256, 512) if diff else (256, 384)
    vcols = (512, 768) if diff else (384, 512)
    wk = kcols[1] - kcols[0]
    dqk = DIFF_QK_DIM if diff else HEAD_DIM
    scale = 1.0 / math.sqrt(dqk)

    @pl.when(qi == 0)
    def _prep():
        kb_ref[0:r, :] = p_ref[:, kcols[0]:kcols[1]].astype(BF16)
        vb_ref[0:r, :] = _values_with_ones(p_ref[:, vcols[0]:vcols[1]])

        @pl.when(jnp.logical_not(is_ctx))
        def _cache():
            kb_ref[r:r + past, :] = ck_ref[0, 0].astype(BF16)
            vb_ref[r:r + past, :] = _values_with_ones(cv_ref[0, 0])

    n_sub = geo["tq_block"] // tq
    lane256 = lax.broadcasted_iota(I32, (tq, 256), 1)
    lane128 = lax.broadcasted_iota(I32, (tq, LANES), 1)

    def q_for(q, u):
        if diff:
            slab = q[:, (u // 4) * LANES:(u // 4 + 1) * LANES]
            lo = (u % 4) * DIFF_QK_DIM
            keep = (lane128 >= lo) & (lane128 < lo + DIFF_QK_DIM)
            return jnp.where(keep, slab, 0.0).astype(BF16), u // 4
        slab = q[:, (u // 2) * LANES:(u // 2 + 1) * LANES]
        g = u // 2
        if (u % 2) != g:
            slab = pltpu.roll(slab, HEAD_DIM, 1)
        keep = (lane128 < HEAD_DIM) if g == 0 else (lane128 >= HEAD_DIM)
        return jnp.where(keep, slab, 0.0).astype(BF16), 0

    def scores(q, u, srcs, sink):
        qm, kslab = q_for(q, u)
        ss = []
        for (st, n, mask) in srcs:
            kk = kb_ref[pl.ds(st, n), kslab * LANES:(kslab + 1) * LANES]
            sc = _dot_nt(qm, kk)
            if mask is not None:
                sc = jnp.where(mask, sc, -jnp.inf)
            ss.append(sc)
        mx = ss[0].max(axis=-1, keepdims=True)
        for sc in ss[1:]:
            mx = jnp.maximum(mx, sc.max(axis=-1, keepdims=True))
        if sink is not None:
            mx = jnp.maximum(mx, sink)
        return ss, mx

    def softmax_pv(ss, mx, vslab, srcs, sink, upper):
        ol = jnp.zeros((tq, LANES), F32)
        for (st, n, _), sc in zip(srcs, ss):
            pexp = jnp.exp((sc - mx).astype(BF16))
            ol = ol + _dot(pexp, vb_ref[pl.ds(st, n), vslab * LANES:(vslab + 1) * LANES])
        if sink is not None:
            ol = ol + jnp.where(lane128 >= HEAD_DIM, jnp.exp(sink - mx), 0.0)
        sw = pltpu.roll(ol, HEAD_DIM, 1)
        return sw / ol if upper else ol / sw

    if diff:
        lam_init = 0.8 - 0.6 * math.exp(-0.3 * li)
        lv = lam_ref[...]
        lam = (jnp.exp(jnp.sum(lv[0:1] * lv[1:2], axis=1, keepdims=True))
               - jnp.exp(jnp.sum(lv[2:3] * lv[3:4], axis=1, keepdims=True)) + lam_init)

    n_u = 8 if diff else 4
    sink_of = lambda u: sink_ref[li, u] if kind == "B" else None

    def finish(outs, j):
        halves = [outs[2 * h] - lam * outs[2 * h + 1] for h in range(4)] if diff else outs
        low = lane128 < HEAD_DIM
        acc = jnp.concatenate([jnp.where(low, halves[0], halves[1]), jnp.where(low, halves[2], halves[3])], axis=1)
        if diff:
            sq = acc * acc
            inv = jnp.zeros((tq, 256), F32)
            for h in range(4):
                head = (lane256 >= h * HEAD_DIM) & (lane256 < (h + 1) * HEAD_DIM)
                ms = jnp.sum(jnp.where(head, sq, 0.0), axis=-1, keepdims=True) * (1.0 / HEAD_DIM)
                inv = jnp.where(head, lax.rsqrt(ms + NORM_EPS), inv)
            acc = acc * inv * gdo_ref[...] * (1.0 - lam_init)
        y_ref[j * tq:(j + 1) * tq, :] = acc

    def run(jobs, batched):
        qs = [p_ref[pl.ds(row0, tq), 0:256] * scale for (row0, _) in jobs]
        items = [(j, u) for j in range(len(jobs)) for u in range(n_u)]
        ahead = len(items) if batched else 2
        issue = lambda j, u: scores(qs[j], u, jobs[j][1], sink_of(u))
        pending = [issue(*it) for it in items[:ahead]]
        outs = [[] for _ in jobs]
        for k, (j, u) in enumerate(items):
            if k + ahead < len(items):
                pending.append(issue(*items[k + ahead]))
            h = u // 2 if diff else u
            outs[j].append(softmax_pv(*pending.pop(0), h if diff else h // 2, jobs[j][1], sink_of(u), h % 2 == 1))
            if u == n_u - 1:
                finish(outs[j], j)

    base = qi * (n_sub * tq)
    rows0 = [pl.multiple_of(base + j * tq, tq) for j in range(n_sub)]

    @pl.when(is_ctx)
    def _ctx():
        run([(q0, [(q0, tq, None)]) for q0 in rows0], True)

    @pl.when(jnp.logical_not(is_ctx))
    def _lat():
        jobs = []
        for q0 in rows0:
            if kind == "B":
                span = tq + 2 * WINDOW
                st = pl.multiple_of(jnp.clip(q0 - WINDOW, 0, r - span), WINDOW)
                qpos = q0 + lax.broadcasted_iota(I32, (tq, span), 0)
                kpos = st + lax.broadcasted_iota(I32, (tq, span), 1)
                jobs.append((q0, [(st, span, jnp.abs(qpos - kpos) <= WINDOW), (r, past, None)]))
            else:
                jobs.append((q0, [(0, r + past, None)]))
        run(jobs, False)


def _attention(kind, geo, li, p, ck, cv, sink, lam, gdo):
    t_all, w = p.shape
    r, past, tq, ng = geo["r"], geo["past"], geo["tq_block"], geo["n_ctx_groups"]
    ns = t_all // r
    wk = 256 if kind == "D" else 128
    cache_idx = lambda s, q: (jnp.maximum(s - ng, 0), li, 0, 0)
    return pl.pallas_call(
        functools.partial(_attn_kernel, kind, geo, li),
        grid=(ns, r // tq),
        in_specs=[pl.BlockSpec((r, w), lambda s, q: (s, 0)),
                  pl.BlockSpec((1, 1, past, ck.shape[-1]), cache_idx),
                  pl.BlockSpec((1, 1, past, cv.shape[-1]), cache_idx),
                  pl.BlockSpec(memory_space=pltpu.SMEM),
                  _layer_spec(lam, li, 2), _layer_spec(gdo, li, 2)],
        out_specs=pl.BlockSpec((tq, 256), lambda s, q: (s * (r // tq) + q, 0)),
        out_shape=jax.ShapeDtypeStruct((t_all, 256), F32),
        scratch_shapes=[pltpu.VMEM((r + past, wk), BF16), pltpu.VMEM((r + past, 2 * wk), BF16)],
        compiler_params=_params(("arbitrary", "arbitrary")),
    )(p, ck, cv, sink, lam, gdo)


def _ssd_kernel(geo, li, p_ref, st0_ref, cw_ref, cb_ref, dtb_ref, alog_ref, dsk_ref, gs_ref,
                y_ref, so_ref, xpad, u_scr, dt_scr, la_scr, y_scr, upd_scr, cd_scr, ex_scr, st_scr, sf_ref, sb_ref):
    r, seq, ng = geo["r"], geo["seq"], geo["n_ctx_groups"]
    cw_ref, cb_ref, dtb_ref, alog_ref, dsk_ref, gs_ref = (
        t.at[0] for t in (cw_ref, cb_ref, dtb_ref, alog_ref, dsk_ref, gs_ref))
    q = SSM_CHUNK
    nh = SSM_HEADS
    is_ctx = pl.program_id(0) < ng
    inner = nh * SSM_HEAD_DIM
    conv_ch = inner + 2 * LANES
    nchunks = r // q
    spg = r // seq
    gap = 8

    def fill(n_seq, length):
        for j in range(n_seq + 1):
            xpad[j * (length + gap):j * (length + gap) + gap, :] = jnp.zeros((gap, conv_ch), F32)
        for j in range(n_seq):
            xpad[gap + j * (length + gap):gap + j * (length + gap) + length, :] = (
                p_ref[j * length:(j + 1) * length, inner:inner + conv_ch])

    @pl.when(is_ctx)
    def _():
        fill(spg, seq)

    @pl.when(jnp.logical_not(is_ctx))
    def _():
        fill(1, r)

    a_neg = -jnp.exp(alog_ref[...])
    seq_shift = jnp.where(is_ctx, int(math.log2(seq // q)), int(math.log2(r // q)))

    def pre(c, carry):
        r0 = pl.multiple_of(c * q, q)
        st = pl.multiple_of(r0 + lax.shift_right_logical(c, seq_shift) * gap, gap)
        big = xpad[pl.ds(st, q + 16), :]
        acc = jnp.zeros((q, conv_ch), F32) + cb_ref[...]
        for j in range(SSM_CONV):
            sh = pltpu.roll(big, (q + 16) - (gap - SSM_CONV // 2 + j), 0)[0:q]
            acc = acc + sh * cw_ref[j:j + 1, :]
        u_scr[pl.ds(r0, q), :] = _silu(acc)
        raw = p_ref[pl.ds(r0, q), inner + conv_ch:inner + conv_ch + LANES] + dtb_ref[...]
        dt = jnp.maximum(raw, 0.0) + jnp.log1p(jnp.exp(-jnp.abs(raw)))
        dt_scr[pl.ds(r0, q), :] = dt
        la_scr[pl.ds(r0, q), :] = dt * a_neg
        return carry

    lax.fori_loop(0, nchunks, pre, 0)

    row_i = lax.broadcasted_iota(I32, (q, q), 0)
    col_i = lax.broadcasted_iota(I32, (q, q), 1)
    keeps = (col_i <= row_i, col_i >= row_i)
    tris = tuple(jnp.where(k, 1.0, 0.0).astype(BF16) for k in keeps)
    lane128 = lax.broadcasted_iota(I32, (q, LANES), 1)
    lane256 = lax.broadcasted_iota(I32, (q, inner), 1)
    heads256 = [(lane256 >= h * SSM_HEAD_DIM) & (lane256 < (h + 1) * SSM_HEAD_DIM) for h in range(nh)]
    lane8 = lax.broadcasted_iota(I32, (8, inner), 1)
    gmask_t = (lax.broadcasted_iota(I32, (LANES, inner), 0) // SSM_STATE
               == lax.broadcasted_iota(I32, (LANES, inner), 1) // (2 * SSM_HEAD_DIM))

    def phase1(c, carry):
        r0 = pl.multiple_of(c * q, q)
        u = u_scr[pl.ds(r0, q), :]
        xs, bm, cm = u[:, 0:inner], u[:, inner:inner + LANES], u[:, inner + LANES:inner + 2 * LANES]
        bm_b = bm.astype(BF16)
        cb = [_dot_nt(jnp.where((lane128 // SSM_STATE) == g, cm, 0.0).astype(BF16), bm_b) for g in range(2)]
        a1, a2, a3 = _split3(la_scr[pl.ds(r0, q), :])
        cums = [_dot(t, a1) + _dot(t, a2) + _dot(t, a3) for t in tris]
        cum = jnp.where(lane128 < nh, cums[0], cums[1])
        pack = jnp.where(lane128 < 2 * nh, cum, pltpu.roll(dt_scr[pl.ds(r0, q), :], 2 * nh, 1))
        pack_t = pack.T
        bm_t = bm.T
        xstack = jnp.concatenate([jnp.where(heads256[h], xs, 0.0).astype(BF16) for h in range(nh)], axis=0)
        ms, bws = [], []
        for d in range(2):
            edge = q - 1 if d == 0 else 0
            m_h, bw_h, ecol, cdx = [], [], [], jnp.zeros((8, inner), F32)
            for h in range(nh):
                c_ = d * nh + h
                colb = jnp.broadcast_to(cum[:, c_:c_ + 1], (q, q))
                crow = pack_t[c_:c_ + 1, :]
                dtrow = pack_t[2 * nh + c_:2 * nh + c_ + 1, :]
                lm = jnp.exp(jnp.where(keeps[d], colb - crow, -jnp.inf))
                m_h.append((cb[h // 2] * lm * dtrow).astype(BF16))
                tot = crow[:, edge:edge + 1]
                bw_h.append((bm_t * (dtrow * jnp.exp(tot - crow))).astype(BF16))
                ecol.append(jnp.exp(colb))
                cdx = jnp.where((lane8 >= h * SSM_HEAD_DIM) & (lane8 < (h + 1) * SSM_HEAD_DIM), jnp.exp(tot), cdx)
            ms.append(jnp.concatenate(m_h, axis=1))
            bws.append(jnp.concatenate(bw_h, axis=1))
            ex_scr[d, pl.ds(r0, q), :] = jnp.concatenate(
                [jnp.where(lane128 < SSM_HEAD_DIM, ecol[0], ecol[1]),
                 jnp.where(lane128 < SSM_HEAD_DIM, ecol[2], ecol[3])], axis=1)
            cd_scr[d, c] = cdx
        res = _dot(jnp.concatenate(ms + bws, axis=0), xstack)
        y_scr[pl.ds(r0, q), :] = res[0:q] + res[q:2 * q]
        upd_scr[0, c] = jnp.where(gmask_t, res[2 * q:3 * q], 0.0)
        upd_scr[1, c] = jnp.where(gmask_t, res[3 * q:4 * q], 0.0)
        return carry

    lax.fori_loop(0, nchunks, phase1, 0, unroll=2)

    srefs = (sf_ref, sb_ref)

    def scan_step(d, c):
        s_in = srefs[d][...]
        st_scr[d, c] = s_in.astype(BF16)
        srefs[d][...] = s_in * cd_scr[d, c][0:1, :] + upd_scr[d, c]

    @pl.when(is_ctx)
    def _ctx():
        cps = seq // q

        def one_seq(j, carry):
            for d in range(2):
                srefs[d][...] = jnp.zeros((LANES, inner), F32)
                for ci in range(cps):
                    scan_step(d, j * cps + (ci if d == 0 else cps - 1 - ci))
                sv = srefs[d][...]
                so_ref[j, d, 0:LANES, :] = sv[:, 0:LANES].T[:, 0:SSM_STATE]
                so_ref[j, d, LANES:inner, :] = pltpu.roll(sv[:, LANES:inner].T, SSM_STATE, 1)[:, 0:SSM_STATE]
            return carry

        lax.fori_loop(0, spg, one_seq, 0)

    @pl.when(jnp.logical_not(is_ctx))
    def _lat():
        for d in range(2):
            srefs[d][...] = st0_ref[0, 0, d]

        def body(i, carry):
            scan_step(0, i)
            scan_step(1, nchunks - 1 - i)
            return carry

        lax.fori_loop(0, nchunks, body, 0)

    def fin(c, carry):
        r0 = pl.multiple_of(c * q, q)
        cm_b = u_scr[pl.ds(r0, q), inner + LANES:inner + 2 * LANES].astype(BF16)
        y = y_scr[pl.ds(r0, q), :] + u_scr[pl.ds(r0, q), 0:inner] * dsk_ref[...]
        for d in range(2):
            y = y + _dot(cm_b, st_scr[d, c]) * ex_scr[d, pl.ds(r0, q), :]
        y = y * _silu(p_ref[pl.ds(r0, q), 0:inner])
        y_ref[pl.ds(r0, q), :] = _rms(y, gs_ref[...])
        return carry

    lax.fori_loop(0, nchunks, fin, 0, unroll=4)


def _ssd(geo, li, pc, st0, cw, cb, dtb, alog, dsk, gs):
    t_all, w = pc.shape
    r, seq, ng = geo["r"], geo["seq"], geo["n_ctx_groups"]
    ns = t_all // r
    spg = r // seq
    nchunks = r // SSM_CHUNK
    inner = SSM_HEADS * SSM_HEAD_DIM
    full = lambda a: _layer_spec(a, li, 1)
    return pl.pallas_call(
        functools.partial(_ssd_kernel, geo, li),
        grid=(ns,),
        in_specs=[pl.BlockSpec((r, w), lambda s: (s, 0)),
                  pl.BlockSpec((1, 1, 2, LANES, inner), lambda s: (jnp.maximum(s - ng, 0), li, 0, 0, 0)),
                  full(cw), full(cb), full(dtb), full(alog), full(dsk), full(gs)],
        out_specs=[pl.BlockSpec((r, inner), lambda s: (s, 0)),
                   pl.BlockSpec((spg, 2, inner, SSM_STATE), lambda s: (jnp.minimum(s, ng - 1), 0, 0, 0))],
        out_shape=[jax.ShapeDtypeStruct((t_all, inner), F32),
                   jax.ShapeDtypeStruct((ng * spg, 2, inner, SSM_STATE), F32)],
        scratch_shapes=[pltpu.VMEM((r + 8 * (spg + 1), 2 * inner), F32), pltpu.VMEM((r, 2 * inner), F32),
                        pltpu.VMEM((r, LANES), F32), pltpu.VMEM((r, LANES), F32),
                        pltpu.VMEM((r, inner), F32),
                        pltpu.VMEM((2, nchunks, LANES, inner), F32), pltpu.VMEM((2, nchunks, 8, inner), F32),
                        pltpu.VMEM((2, r, inner), F32), pltpu.VMEM((2, nchunks, LANES, inner), BF16),
                        pltpu.VMEM((LANES, inner), F32), pltpu.VMEM((LANES, inner), F32)],
        compiler_params=_params(("arbitrary",)),
    )(pc, st0, cw, cb, dtb, alog, dsk, gs)


def _merge_kernel(nct, xc_ref, xl_ref, mod_ref, g1_ref, g2_ref, ya_ref, yb_ref, yc_ref, yd_ref, wg_ref, wbr_ref,
                  wo_ref, wr_ref, x1_ref, h2_ref, aff_ref):
    g1_ref, g2_ref, wg_ref, wbr_ref, wo_ref, wr_ref = (
        t.at[0] for t in (g1_ref, g2_ref, wg_ref, wbr_ref, wo_ref, wr_ref))
    tm, d = xc_ref.shape
    is_ctx = pl.program_id(0) < nct
    m = mod_ref[0]
    w_hi, w_lo = _split2(wr_ref[...])
    n_sub = 2
    ts = tm // n_sub

    def mix(i):
        rows = slice(i * ts, (i + 1) * ts)
        x = jnp.where(is_ctx, xc_ref[rows, :], xl_ref[rows, :])
        hb = (_rms(x, g1_ref[...]) * (1.0 + m[1:2]) + m[0:1]).astype(BF16)
        merged = jnp.zeros(x.shape, F32)
        for n, y_ref in enumerate((ya_ref, yb_ref, yc_ref, yd_ref)):
            gate = jax.nn.sigmoid(_dot(hb, wg_ref[:, n * d:(n + 1) * d]))
            merged = merged + gate * _dot(y_ref[rows, :].astype(BF16), wbr_ref[n])
        return x + m[2:3] * _dot(merged.astype(BF16), wo_ref[...])

    def tail(i, x1):
        rows = slice(i * ts, (i + 1) * ts)
        x1_ref[rows, :] = x1
        h2 = _rms(x1, g2_ref[...]) * (1.0 + m[4:5]) + m[3:4]
        h2_ref[rows, :] = h2.astype(BF16)
        h_hi, h_lo = _split2(h2)
        logit = _dot_nt(w_hi, h_hi) + _dot_nt(w_hi, h_lo) + _dot_nt(w_lo, h_hi)
        e = jnp.exp(logit - logit.max(axis=0, keepdims=True))
        aff_ref[:, rows] = e / e.sum(axis=0, keepdims=True)

    x1_prev = mix(0)
    for i in range(1, n_sub):
        x1_next = mix(i)
        tail(i - 1, x1_prev)
        x1_prev = x1_next
    tail(n_sub - 1, x1_prev)


def _merge(x, mod, g1, g2, ys, wg, wbr, wo, wr_t, geo, tm, li):
    d = x[0].shape[1]
    t_all = x[0].shape[0] + x[1].shape[0]
    ne = wr_t.shape[1]
    nct = geo["t_ctx"] // tm
    tps = geo["r"] // tm
    mod_row = lambda i: li * MOD_ROWS + jnp.where(i < nct, 0, 1 + (i - nct) // tps)
    full = lambda a: _layer_spec(a, li, 1)
    ytile = pl.BlockSpec((tm, BRANCH_W), lambda i: (i, 0))
    return pl.pallas_call(
        functools.partial(_merge_kernel, nct),
        grid=(t_all // tm,),
        in_specs=_x_specs(nct, tm, d) + [
                  pl.BlockSpec((1, 6, d), lambda i: (mod_row(i), 0, 0)),
                  full(g1), full(g2), ytile, ytile, ytile, ytile, full(wg), full(wbr), full(wo), full(wr_t)],
        out_specs=[pl.BlockSpec((tm, d), lambda i: (i, 0)),
                   pl.BlockSpec((tm, d), lambda i: (i, 0)),
                   pl.BlockSpec((ne, tm), lambda i: (0, i))],
        out_shape=[jax.ShapeDtypeStruct((t_all, d), F32),
                   jax.ShapeDtypeStruct((t_all, d), BF16),
                   jax.ShapeDtypeStruct((ne, t_all), F32)],
        compiler_params=_params(("arbitrary",)),
    )(x[0], x[1], mod, g1, g2, *ys, wg, wbr, wo, wr_t)


def _kth_largest_bits(a, cap):
    def keeps(cand):
        cnt = jnp.sum(jnp.where(a >= lax.bitcast_convert_type(cand, F32), 1.0, 0.0), axis=1, keepdims=True)
        return cnt >= cap

    def body(i, thr):
        lo = 29 - 2 * i
        c1, c2, c3 = (thr | lax.shift_left(jnp.int32(k), lo) for k in (1, 2, 3))
        return jnp.where(keeps(c3), c3, jnp.where(keeps(c2), c2, jnp.where(keeps(c1), c1, thr)))

    thr = lax.fori_loop(0, 15, body, jnp.zeros((a.shape[0], 1), I32))
    return jnp.where(keeps(thr | 1), thr | 1, thr)


def _route_kernel(geo, aff_ref, slot_ref, start_ref):
    r, seq, ng = geo["r"], geo["seq"], geo["n_ctx_groups"]
    ne = aff_ref.shape[0]
    blk = ROUTE_TILE
    s = pl.program_id(0)
    lane_s = lax.broadcasted_iota(I32, (ne, LANES), 1)
    upper = jnp.where(lax.broadcasted_iota(I32, (blk, blk), 0) < lax.broadcasted_iota(I32, (blk, blk), 1),
                      1.0, 0.0).astype(BF16)

    def select(a, cap, nblk):
        thr = _kth_largest_bits(a, cap)
        gt = a >= lax.bitcast_convert_type(thr + 1, F32)
        eq = (a >= lax.bitcast_convert_type(thr, F32)) & jnp.logical_not(gt)
        need = cap - jnp.sum(jnp.where(gt, 1.0, 0.0), axis=1, keepdims=True)
        eq_f = jnp.where(eq, 1.0, 0.0)
        outs, firsts = [], []
        carry_e = jnp.zeros((a.shape[0], 1), F32)
        carry_s = jnp.zeros((a.shape[0], 1), F32)
        for b in range(nblk):
            sl = slice(b * blk, (b + 1) * blk)
            pe = _dot(eq_f[:, sl].astype(BF16), upper) + carry_e
            sel = gt[:, sl] | (eq[:, sl] & (pe < need))
            sel_f = jnp.where(sel, 1.0, 0.0)
            ps = _dot(sel_f.astype(BF16), upper) + carry_s
            outs.append(jnp.where(sel, ps, -1.0))
            firsts.append(carry_s)
            carry_e = carry_e + eq_f[:, sl].sum(axis=1, keepdims=True)
            carry_s = carry_s + sel_f.sum(axis=1, keepdims=True)
        return outs, firsts

    @pl.when(s < ng)
    def _ctx():
        spg = r // seq
        cap = EC_FACTOR * seq // ne
        a = jnp.concatenate([aff_ref[:, j * seq:(j + 1) * seq] for j in range(spg)], axis=0)
        out, _ = select(a, cap, 1)
        for j in range(spg):
            o = out[0][j * ne:(j + 1) * ne]
            slot_ref[:, j * seq:(j + 1) * seq] = jnp.where(o >= 0, o + j * cap, -1.0).astype(I32)
        start_ref[...] = jnp.minimum(lane_s, spg) * cap

    @pl.when(s >= ng)
    def _lat():
        cap = EC_FACTOR * r // ne
        out, firsts = select(aff_ref[...], cap, r // blk)
        st = jnp.full((ne, LANES), float(cap), F32)
        for b, o in enumerate(out):
            slot_ref[:, b * blk:(b + 1) * blk] = o.astype(I32)
            st = jnp.where(lane_s == b, firsts[b], st)
        start_ref[...] = st.astype(I32)


def _route(aff_t, geo):
    ne, t_all = aff_t.shape
    r = geo["r"]
    return pl.pallas_call(
        functools.partial(_route_kernel, geo),
        grid=(t_all // r,),
        in_specs=[pl.BlockSpec((ne, r), lambda s: (0, s))],
        out_specs=[pl.BlockSpec((ne, r), lambda s: (0, s)), pl.BlockSpec((ne, LANES), lambda s: (0, s))],
        out_shape=[jax.ShapeDtypeStruct((ne, t_all), I32), jax.ShapeDtypeStruct((ne, (t_all // r) * LANES), I32)],
        compiler_params=_params(("arbitrary",)),
    )(aff_t)


def _tile_starts(starts_ref, e, s, ns, nt):
    base = (e * ns + s) * (nt + 1)
    return [starts_ref[base + i] for i in range(nt + 1)]


def _expert_kernel(starts_ref, x_ref, slot_ref, aff_ref, w1_ref, w3_ref, w2_ref, yo_ref, wb1, wb3, wb2,
                   xs_scr, gs_scr):
    e, s, ns = pl.program_id(0), pl.program_id(1), pl.num_programs(1)

    @pl.when(s == 0)
    def _cast():
        wb1[...] = w1_ref[0, 0].astype(BF16)
        wb3[...] = w3_ref[0, 0].astype(BF16)
        wb2[...] = w2_ref[0, 0].astype(BF16)

    cap = yo_ref.shape[2]
    r = x_ref.shape[0]
    nt = r // ROUTE_TILE
    win = min(GATHER_WINDOW, cap)
    slot = slot_ref[0]
    aff = aff_ref[0]
    a = _tile_starts(starts_ref, e, s, ns, nt)
    fits = a[1] - a[0] <= win - 16
    for i in range(1, nt):
        fits = fits & (a[i + 1] - a[i] <= win - 16)

    @pl.when(fits)
    def _windows():
        xs_scr[...] = jnp.zeros(xs_scr.shape, F32)
        gs_scr[...] = jnp.zeros(gs_scr.shape, F32)
        for i in range(nt):
            cols = slice(i * ROUTE_TILE, (i + 1) * ROUTE_TILE)
            w0 = pl.multiple_of(jnp.minimum((a[i] // 16) * 16, cap - win), 16)
            pick = (lax.broadcasted_iota(I32, (win, ROUTE_TILE), 0) + w0) == slot[:, cols]
            xs_scr[pl.ds(w0, win), :] += _dot(jnp.where(pick, 1.0, 0.0).astype(BF16), x_ref[cols, :])
            gs_scr[pl.ds(w0, win), :] += jnp.sum(jnp.where(pick, aff[:, cols], 0.0), axis=1, keepdims=True)

    @pl.when(jnp.logical_not(fits))
    def _all_rows():
        pick = lax.broadcasted_iota(I32, (cap, r), 0) == slot
        xs_scr[...] = _dot(jnp.where(pick, 1.0, 0.0).astype(BF16), x_ref[...])
        gs_scr[...] = jnp.broadcast_to(jnp.sum(jnp.where(pick, aff, 0.0), axis=1, keepdims=True), gs_scr.shape)

    xs = xs_scr[...].astype(BF16)
    hid = _silu(_dot(xs, wb1[...])) * _dot(xs, wb3[...])
    yo_ref[0, 0] = (_dot(hid.astype(BF16), wb2[...]) * gs_scr[:, 0:1]).astype(BF16)


def _experts(starts, h2, slot3, aff3, w1, w3, w2, geo, li):
    t_all, d = h2.shape
    _, ne, _, f = w1.shape
    r = geo["r"]
    ns = t_all // r
    cap = EC_FACTOR * r // ne
    wspec = lambda a: pl.BlockSpec((1, 1) + a.shape[2:], lambda e, s, st: (li, e, 0, 0))
    return pl.pallas_call(
        _expert_kernel,
        grid_spec=pltpu.PrefetchScalarGridSpec(
            num_scalar_prefetch=1,
            grid=(ne, ns),
            in_specs=[pl.BlockSpec((r, d), lambda e, s, st: (s, 0)),
                      pl.BlockSpec((1, 1, r), lambda e, s, st: (e, 0, s)),
                      pl.BlockSpec((1, 1, r), lambda e, s, st: (e, 0, s)),
                      wspec(w1), wspec(w3), wspec(w2)],
            out_specs=pl.BlockSpec((1, 1, cap, d), lambda e, s, st: (s, e, 0, 0)),
            scratch_shapes=[pltpu.VMEM((d, f), BF16), pltpu.VMEM((d, f), BF16), pltpu.VMEM((f, d), BF16),
                            pltpu.VMEM((cap, d), F32), pltpu.VMEM((cap, LANES), F32)]),
        out_shape=jax.ShapeDtypeStruct((ns, ne, cap, d), BF16),
        compiler_params=_params(("arbitrary", "arbitrary")),
    )(starts, h2, slot3, aff3, w1, w3, w2)


def _scatter_kernel(ng, starts_ref, x_ref, mod_ref, slot_ref, yo_ref, oc_ref, ol_ref, ywin, ffn_scr):
    tt = x_ref.shape[0]
    _, ne, cap, d = yo_ref.shape
    s, i, ns, nt = pl.program_id(0), pl.program_id(1), pl.num_programs(0), pl.num_programs(1)
    win = min(SCATTER_WINDOW, cap)
    slot = slot_ref[...]
    first = [starts_ref[(e * ns + s) * (nt + 1) + i] for e in range(ne)]
    last = [starts_ref[(e * ns + s) * (nt + 1) + i + 1] for e in range(ne)]
    fits = last[0] - first[0] <= win - 16
    for e in range(1, ne):
        fits = fits & (last[e] - first[e] <= win - 16)

    @pl.when(fits)
    def _windows():
        lane = lax.broadcasted_iota(I32, (tt, win), 1)
        hot = []
        for e in range(ne):
            w0 = pl.multiple_of(jnp.minimum((first[e] // 16) * 16, cap - win), 16)
            ywin[e * win:(e + 1) * win, :] = yo_ref[0, e, pl.ds(w0, win), :]
            hot.append(jnp.where(slot[:, e:e + 1] - w0 == lane, 1.0, 0.0).astype(BF16))
        ffn_scr[...] = _dot(jnp.concatenate(hot, axis=1), ywin[...])

    @pl.when(jnp.logical_not(fits))
    def _all_rows():
        lane = lax.broadcasted_iota(I32, (tt, cap), 1)
        hot = jnp.concatenate([jnp.where(slot[:, e:e + 1] == lane, 1.0, 0.0).astype(BF16) for e in range(ne)], axis=1)
        ffn_scr[...] = _dot(hot, yo_ref[0].reshape(ne * cap, d))

    out = x_ref[...] + mod_ref[0][5:6] * ffn_scr[...]
    is_ctx = pl.program_id(0) < ng

    @pl.when(is_ctx)
    def _():
        oc_ref[...] = out

    @pl.when(jnp.logical_not(is_ctx))
    def _():
        ol_ref[...] = out


def _scatter(starts, x1, mod, slot_t, yo, geo, li):
    t_all, d = x1.shape
    ns, ne, cap, _ = yo.shape
    r = geo["r"]
    ng = geo["n_ctx_groups"]
    tt = ROUTE_TILE
    tpg = r // tt
    nct = ng * tpg
    win = min(SCATTER_WINDOW, cap)
    mod_row = lambda s, i, st: li * MOD_ROWS + jnp.where(s < ng, 0, 1 + s - ng)
    out_specs = [pl.BlockSpec((tt, d), lambda s, i, st: (jnp.minimum(s * tpg + i, nct - 1), 0)),
                 pl.BlockSpec((tt, d), lambda s, i, st: (jnp.maximum(s * tpg + i - nct, 0), 0))]
    out_shape = [jax.ShapeDtypeStruct((nct * tt, d), F32), jax.ShapeDtypeStruct((t_all - nct * tt, d), F32)]
    return pl.pallas_call(
        functools.partial(_scatter_kernel, ng),
        grid_spec=pltpu.PrefetchScalarGridSpec(
            num_scalar_prefetch=1,
            grid=(ns, tpg),
            in_specs=[pl.BlockSpec((tt, d), lambda s, i, st: (s * tpg + i, 0)),
                      pl.BlockSpec((1, 6, d), lambda s, i, st: (mod_row(s, i, st), 0, 0)),
                      pl.BlockSpec((tt, ne), lambda s, i, st: (s * tpg + i, 0)),
                      pl.BlockSpec((1, ne, cap, d), lambda s, i, st: (s, 0, 0, 0))],
            out_specs=out_specs,
            scratch_shapes=[pltpu.VMEM((ne * win, d), BF16), pltpu.VMEM((tt, d), F32)]),
        out_shape=out_shape,
        compiler_params=_params(("arbitrary", "arbitrary")),
    )(starts, x1, mod, slot_t, yo)


def _rope_table(s, d, n_qk_lanes, n_v_lanes, pad_rows):
    n = d // 4
    t = np.arange(s)
    rows = (t // GRID_W).astype(np.float64)
    cols = (t % GRID_W).astype(np.float64)
    inv = ROPE_THETA ** (-np.arange(n, dtype=np.float64) / n)
    ang = np.stack([rows[:, None] * inv, cols[:, None] * inv], axis=1)
    cos, sin = np.cos(ang), np.sin(ang)
    cos_h = np.concatenate([cos[:, 0], cos[:, 0], cos[:, 1], cos[:, 1]], axis=-1)
    sin_h = np.concatenate([-sin[:, 0], sin[:, 0], -sin[:, 1], sin[:, 1]], axis=-1)
    reps = n_qk_lanes // d
    cos_f = np.concatenate([np.tile(cos_h, (1, reps)), np.ones((s, n_v_lanes))], axis=1)
    sin_f = np.concatenate([np.tile(sin_h, (1, reps)), np.zeros((s, n_v_lanes))], axis=1)
    w = n_qk_lanes + n_v_lanes
    cos_f = np.concatenate([cos_f, np.ones((pad_rows, w))], axis=0)
    sin_f = np.concatenate([sin_f, np.zeros((pad_rows, w))], axis=0)
    return jnp.asarray(cos_f.astype(np.float32)), jnp.asarray(sin_f.astype(np.float32))


def _block_diag_ones(w, gsize):
    i = np.arange(w)
    return jnp.asarray((i[:, None] // gsize == i[None, :] // gsize).astype(np.float32), dtype=BF16)


def kernel(x_prompt, x_sample, c, cache_a_k, cache_a_v, cache_b_k, cache_b_v, state_ssm, cache_d_k, cache_d_v, c_ctx, w_ada, b_ada, g_norm1, g_norm2, w_in, g_qa, g_ka, g_qb, g_kb, sink_b, conv_w, conv_b, dt_bias, a_log, d_skip, g_ssm, g_qd, g_kd, lam_q1, lam_k1, lam_q2, lam_k2, g_dout, w_br, w_out, w_router, w_e1, w_e3, w_e2):
    batch, seq, d = x_prompt.shape
    dec_batch, dec_seq, _ = x_sample.shape
    depth = w_in.shape[0]
    past = cache_a_k.shape[2]
    ne = w_router.shape[2]
    r = dec_seq
    t_ctx = batch * seq
    assert t_ctx % r == 0 and r % seq == 0 and seq == 256 and past % 16 == 0
    tm = 512
    assert t_ctx % tm == 0 and r % tm == 0
    geo = dict(r=r, seq=seq, past=past, tq=256, tq_block=512, t_ctx=t_ctx, n_ctx_groups=t_ctx // r)
    ng = geo["n_ctx_groups"]

    cosa, sina = _rope_table(r, HEAD_DIM, 384, 128, tm)
    cosd, sind = _rope_table(r, DIFF_QK_DIM, 512, 256, tm)
    bda = _block_diag_ones(256, HEAD_DIM)
    bdd = _block_diag_ones(256, DIFF_QK_DIM)
    inner = SSM_HEADS * SSM_HEAD_DIM
    gsel = (np.arange(LANES)[:, None] // SSM_STATE) == (np.arange(inner)[None, :] // (2 * SSM_HEAD_DIM))

    cvec = jnp.concatenate([c_ctx[None, :], c, jnp.zeros((MOD_ROWS - 1 - dec_batch, d), F32)], axis=0)
    mod_all = _adaln(cvec, w_ada, b_ada)

    x = (x_prompt.reshape(t_ctx, d), x_sample.reshape(dec_batch * r, d))
    ck_a = cache_a_k.reshape(dec_batch, depth, past, -1)
    cv_a = cache_a_v.reshape(dec_batch, depth, past, -1)
    ck_b = cache_b_k.reshape(dec_batch, depth, past, -1)
    cv_b = cache_b_v.reshape(dec_batch, depth, past, -1)
    ck_d = cache_d_k.reshape(dec_batch, depth, past, -1)
    cv_d = cache_d_v.reshape(dec_batch, depth, past, -1)
    st0 = jnp.moveaxis(state_ssm, -1, -3).reshape(dec_batch, depth, 2, SSM_STATE, inner)
    st0 = jnp.where(gsel, jnp.concatenate([st0, st0], axis=-2), 0.0)

    mod = mod_all.reshape(depth * MOD_ROWS, 6, d)
    w_mix = jnp.concatenate([w_in[:, :, :DT_SRC], jnp.zeros((depth, d, COL_C[1] - COL_C[0] - 776), F32),
                             w_in[:, :, DT_SRC:N_MIX_SRC]], axis=2).astype(BF16)
    w_gate = w_in[:, :, N_MIX_SRC:].astype(BF16)
    w_br_b, w_out_b = w_br.astype(BF16), w_out.astype(BF16)
    w_r_t = jnp.swapaxes(w_router, 1, 2)
    lay = lambda *parts: jnp.concatenate(parts, axis=-1)[:, None, :]
    rep = lambda g, n: jnp.tile(g, (1, n))
    consts = dict(
        bda=bda, bdd=bdd, cosa=cosa, sina=sina, cosd=cosd, sind=sind,
        ga=lay(rep(g_qa, 4), rep(g_ka, 2), jnp.ones((depth, 128), F32)),
        gb=lay(rep(g_qb, 4), rep(g_kb, 2), jnp.ones((depth, 128), F32)),
        gd=lay(rep(g_qd, 8), rep(g_kd, 8), jnp.ones((depth, 256), F32)))
    g1, g2 = g_norm1[:, None, :], g_norm2[:, None, :]
    lam = jnp.stack([lam_q1, lam_k1, lam_q2, lam_k2], axis=1)
    gdo = rep(g_dout, 4)[:, None, :]
    pad3 = lambda a, rows, lanes: jnp.pad(a, ((0, 0), (0, rows - a.shape[1]), (0, lanes - a.shape[2])))
    ssd_par = (pad3(conv_w, 8, conv_w.shape[2]), conv_b[:, None, :],
               pad3(dt_bias.reshape(depth, 1, -1), 1, LANES), pad3(a_log.reshape(depth, 1, -1), 1, LANES),
               jnp.repeat(d_skip, SSM_HEAD_DIM, axis=1)[:, None, :], g_ssm[:, None, :])

    new = {k: [] for k in ("a_k", "a_v", "b_k", "b_v", "ssm", "d_k", "d_v")}
    for li in range(depth):
        pa, pb, pc, pd = _inproj(x, mod, g1, w_mix, consts, geo, tm, li)
        ya = _attention("A", geo, li, pa, ck_a, cv_a, sink_b, lam, gdo)
        yb = _attention("B", geo, li, pb, ck_b, cv_b, sink_b, lam, gdo)
        yd = _attention("D", geo, li, pd, ck_d, cv_d, sink_b, lam, gdo)
        yc, ssm_new = _ssd(geo, li, pc, st0, *ssd_par)

        x1, h2, aff_t = _merge(x, mod, g1, g2, (ya, yb, yc, yd), w_gate, w_br_b, w_out_b, w_r_t, geo, tm, li)
        slot, starts = _route(aff_t, geo)
        starts = starts.reshape(ne, -1, LANES)[:, :, :r // ROUTE_TILE + 1].reshape(-1)
        yo = _experts(starts, h2, slot.reshape(ne, 1, -1), aff_t.reshape(ne, 1, -1), w_e1, w_e3, w_e2, geo, li)
        x = _scatter(starts, x1, mod, slot.T, yo, geo, li)

        ctx = lambda p, lo, hi: p[:t_ctx, lo:hi]
        new["a_k"].append(ctx(pa, 256, 384).reshape(batch, seq, 2, HEAD_DIM))
        new["a_v"].append(ctx(pa, 384, 512).reshape(batch, seq, 2, HEAD_DIM))
        new["b_k"].append(ctx(pb, 256, 384).reshape(batch, seq, 2, HEAD_DIM))
        new["b_v"].append(ctx(pb, 384, 512).reshape(batch, seq, 2, HEAD_DIM))
        new["d_k"].append(ctx(pd, 256, 512).reshape(batch, seq, 4, 2, DIFF_QK_DIM))
        new["d_v"].append(ctx(pd, 512, 768).reshape(batch, seq, 4, HEAD_DIM))
        new["ssm"].append(ssm_new.reshape(batch, 2, SSM_HEADS, SSM_HEAD_DIM, SSM_STATE))

    st = lambda k: jnp.stack(new[k], axis=1)
    return (x[0].reshape(batch, seq, d), x[1].reshape(dec_batch, r, d),
            st("a_k"), st("a_v"), st("b_k"), st("b_v"), st("ssm"), st("d_k"), st("d_v"))
```

```python
import functools
import math

import numpy as np
import jax
import jax.numpy as jnp
from jax import lax
from jax.experimental import pallas as pl
from jax.experimental.pallas import tpu as pltpu

F32 = jnp.float32
BF16 = jnp.bfloat16
I32 = jnp.int32

HEAD_DIM = 64
DIFF_QK_DIM = 32
GRID_W = 64
ROPE_THETA = 10000.0
NORM_EPS = 1e-6
WINDOW = 128
SSM_CHUNK = 128
SSM_HEADS = 4
SSM_HEAD_DIM = 64
SSM_STATE = 64
SSM_CONV = 5
EC_FACTOR = 2
BRANCH_W = 256
N_BRANCH = 4

ROUTE_TILE = 256
GATHER_WINDOW = 80
SCATTER_WINDOW = 128

MOD_ROWS = 16
LANES = 128
VMEM_LIMIT = 56 * 1024 * 1024

COL_A = (0, 512)
COL_B = (512, 1024)
COL_C = (1024, 1920)
COL_D = (1920, 2688)
W_MIX = 2688
N_MIX_SRC = 2568
DT_SRC = 1800


def _dot(a, b):
    return jnp.dot(a, b, preferred_element_type=F32)


def _dot_nt(a, b):
    return lax.dot_general(a, b, (((1,), (1,)), ((), ())), preferred_element_type=F32)


def _dot_tn(a, b):
    return lax.dot_general(a, b, (((0,), (0,)), ((), ())), preferred_element_type=F32)


def _split2(x):
    hi = x.astype(BF16)
    lo = (x - hi.astype(F32)).astype(BF16)
    return hi, lo


def _split3(x):
    hi = x.astype(BF16)
    r = x - hi.astype(F32)
    mid = r.astype(BF16)
    lo = (r - mid.astype(F32)).astype(BF16)
    return hi, mid, lo


def _silu(x):
    return x * jax.nn.sigmoid(x)


def _rms(x, g):
    ms = jnp.mean(x * x, axis=-1, keepdims=True)
    return x * lax.rsqrt(ms + NORM_EPS) * g


def _params(sem):
    return pltpu.CompilerParams(dimension_semantics=sem, vmem_limit_bytes=VMEM_LIMIT)


def _adaln_kernel(c_ref, w_ref, b_ref, o_ref):
    s_hi, s_lo = _split2(_silu(c_ref[...]))
    w_hi, w_lo = _split2(w_ref[0])
    o_ref[0] = _dot(s_hi, w_hi) + _dot(s_hi, w_lo) + _dot(s_lo, w_hi) + b_ref[0]


def _adaln(cvec, w_ada, b_ada):
    depth, d, n = w_ada.shape
    m = cvec.shape[0]
    tn = 1024
    return pl.pallas_call(
        _adaln_kernel,
        grid=(depth, n // tn),
        in_specs=[pl.BlockSpec((m, d), lambda l, j: (0, 0)),
                  pl.BlockSpec((1, d, tn), lambda l, j: (l, 0, j)),
                  pl.BlockSpec((1, 1, tn), lambda l, j: (l, 0, j))],
        out_specs=pl.BlockSpec((1, m, tn), lambda l, j: (l, 0, j)),
        out_shape=jax.ShapeDtypeStruct((depth, m, n), F32),
        compiler_params=_params(("arbitrary", "arbitrary")),
    )(cvec, w_ada, b_ada.reshape(depth, 1, n))


def _qknorm_rope_store(p, bd, gain, cos_ref, sin_ref, segs, gsize, n, o_ref):
    parts = []
    done = 0
    for (lo, hi) in segs:
        ps = p[:, lo:hi]
        ssum = _dot((ps * ps).astype(BF16), bd[0:hi - lo, 0:hi - lo])
        parts.append(ps * lax.rsqrt(ssum * (1.0 / gsize) + NORM_EPS) * gain[:, lo:hi])
        done = hi
    parts.append(p[:, done:])
    y = jnp.concatenate(parts, axis=1)
    for c in range(p.shape[1] // LANES):
        sl = slice(c * LANES, (c + 1) * LANES)
        ys = y[:, sl]
        up = pltpu.roll(ys, LANES - n, 1)
        dn = pltpu.roll(ys, n, 1)
        l128 = lax.broadcasted_iota(I32, ys.shape, 1)
        sw = jnp.where((l128 & (2 * n - 1)) < n, up, dn)
        o_ref[:, sl] = ys * cos_ref[:, sl] + sw * sin_ref[:, sl]


def _pick_x(nct, xc_ref, xl_ref):
    return jnp.where(pl.program_id(0) < nct, xc_ref[...], xl_ref[...])


def _inproj_kernel(nct, xc_ref, xl_ref, mod_ref, g1_ref, w_ref, bda_ref, bdd_ref, ga_ref, gb_ref, gd_ref,
                   cosa_ref, sina_ref, cosd_ref, sind_ref, pa_ref, pb_ref, pc_ref, pd_ref):
    g1_ref, w_ref, ga_ref, gb_ref, gd_ref = (t.at[0] for t in (g1_ref, w_ref, ga_ref, gb_ref, gd_ref))
    m = mod_ref[0]
    h = _rms(_pick_x(nct, xc_ref, xl_ref), g1_ref[...]) * (1.0 + m[1:2]) + m[0:1]
    hb = h.astype(BF16)
    pa = _dot(hb, w_ref[:, COL_A[0]:COL_A[1]])
    pb = _dot(hb, w_ref[:, COL_B[0]:COL_B[1]])
    _qknorm_rope_store(pa, bda_ref[...], ga_ref[...], cosa_ref, sina_ref, ((0, 256), (256, 384)), HEAD_DIM, HEAD_DIM // 4, pa_ref)
    pd = _dot(hb, w_ref[:, COL_D[0]:COL_D[1]])
    _qknorm_rope_store(pb, bda_ref[...], gb_ref[...], cosa_ref, sina_ref, ((0, 256), (256, 384)), HEAD_DIM, HEAD_DIM // 4, pb_ref)
    pc_ref[...] = _dot(hb, w_ref[:, COL_C[0]:COL_C[1]])
    _qknorm_rope_store(pd, bdd_ref[...], gd_ref[...], cosd_ref, sind_ref, ((0, 256), (256, 512)), DIFF_QK_DIM, DIFF_QK_DIM // 4, pd_ref)


def _layer_spec(a, li, n_grid):
    zeros = (0,) * (a.ndim - 1)
    if n_grid == 1:
        return pl.BlockSpec((1,) + a.shape[1:], lambda i: (li,) + zeros, pipeline_mode=pl.Buffered(1))
    return pl.BlockSpec((1,) + a.shape[1:], lambda i, j: (li,) + zeros, pipeline_mode=pl.Buffered(1))


def _inproj(x, mod, g1, w_mix, consts, geo, tm, li):
    d = x[0].shape[1]
    t_all = x[0].shape[0] + x[1].shape[0]
    nct = geo["t_ctx"] // tm
    tps = geo["r"] // tm
    nb = x[1].shape[0] // geo["r"]
    seq_of = lambda i: (i - nct) % nb
    pos_of = lambda i: (i - nct) // nb
    lat_tile = lambda i: seq_of(i) * tps + pos_of(i)

    def mod_row(i):
        return li * MOD_ROWS + jnp.where(i < nct, 0, 1 + seq_of(i))

    def rope_row(i):
        return jnp.where(i < nct, tps, pos_of(i))

    full = lambda a: pl.BlockSpec(a.shape, lambda i: (0,) * a.ndim, pipeline_mode=pl.Buffered(1))
    layer = lambda a: _layer_spec(a, li, 1)
    widths = [COL_A[1] - COL_A[0], COL_B[1] - COL_B[0], COL_C[1] - COL_C[0], COL_D[1] - COL_D[0]]
    tab = lambda w: pl.BlockSpec((tm, w), lambda i: (rope_row(i), 0))
    return pl.pallas_call(
        functools.partial(_inproj_kernel, nct),
        grid=(t_all // tm,),
        in_specs=[pl.BlockSpec((tm, d), lambda i: (jnp.minimum(i, nct - 1), 0)),
                  pl.BlockSpec((tm, d), lambda i: (jnp.where(i < nct, 0, lat_tile(i)), 0)),
                  pl.BlockSpec((1, 6, d), lambda i: (mod_row(i), 0, 0)),
                  layer(g1), layer(w_mix), full(consts["bda"]), full(consts["bdd"]),
                  layer(consts["ga"]), layer(consts["gb"]), layer(consts["gd"]),
                  tab(512), tab(512), tab(768), tab(768)],
        out_specs=[pl.BlockSpec((tm, w), lambda i: (jnp.where(i < nct, i, nct + lat_tile(i)), 0)) for w in widths],
        out_shape=[jax.ShapeDtypeStruct((t_all, w), F32) for w in widths],
        compiler_params=_params(("arbitrary",)),
    )(x[0], x[1], mod, g1, w_mix, consts["bda"], consts["bdd"], consts["ga"], consts["gb"], consts["gd"],
      consts["cosa"], consts["sina"], consts["cosd"], consts["sind"])


def _values_with_ones(v):
    lane = lax.broadcasted_iota(I32, (v.shape[0], LANES), 1)
    slabs = []
    for c in range(v.shape[1] // LANES):
        vs = v[:, c * LANES:(c + 1) * LANES]
        slabs.append(jnp.where(lane < HEAD_DIM, vs, 1.0))
        slabs.append(jnp.where(lane < HEAD_DIM, pltpu.roll(vs, HEAD_DIM, 1), 1.0))
    return jnp.concatenate(slabs, axis=1).astype(BF16)


def _attn_kernel(kind, geo, li, p_ref, ck_ref, cv_ref, sink_ref, lam_ref, gdo_ref, y_ref, kb_ref, vb_ref):
    r, past, tq, ng = geo["r"], geo["past"], geo["tq"], geo["n_ctx_groups"]
    lam_ref, gdo_ref = lam_ref.at[0], gdo_ref.at[0]
    s = pl.program_id(0)
    qi = pl.program_id(1)
    is_ctx = s < ng
    diff = kind == "D"
    kcols = (256, 512) if diff else (256, 384)
    vcols = (512, 768) if diff else (384, 512)
    wk = kcols[1] - kcols[0]
    dqk = DIFF_QK_DIM if diff else HEAD_DIM
    scale = 1.0 / math.sqrt(dqk)

    @pl.when(qi == 0)
    def _prep():
        kb_ref[0:r, :] = p_ref[:, kcols[0]:kcols[1]].astype(BF16)
        vb_ref[0:r, :] = _values_with_ones(p_ref[:, vcols[0]:vcols[1]])

        @pl.when(jnp.logical_not(is_ctx))
        def _cache():
            kb_ref[r:r + past, :] = ck_ref[0, 0].astype(BF16)
            vb_ref[r:r + past, :] = _values_with_ones(cv_ref[0, 0])

    n_sub = geo["tq_block"] // tq
    lane256 = lax.broadcasted_iota(I32, (tq, 256), 1)
    lane128 = lax.broadcasted_iota(I32, (tq, LANES), 1)

    def q_for(q, u):
        if diff:
            slab = q[:, (u // 4) * LANES:(u // 4 + 1) * LANES]
            lo = (u % 4) * DIFF_QK_DIM
            keep = (lane128 >= lo) & (lane128 < lo + DIFF_QK_DIM)
            return jnp.where(keep, slab, 0.0).astype(BF16), u // 4
        slab = q[:, (u // 2) * LANES:(u // 2 + 1) * LANES]
        g = u // 2
        if (u % 2) != g:
            slab = pltpu.roll(slab, HEAD_DIM, 1)
        keep = (lane128 < HEAD_DIM) if g == 0 else (lane128 >= HEAD_DIM)
        return jnp.where(keep, slab, 0.0).astype(BF16), 0

    def scores(q, u, srcs, sink):
        qm, kslab = q_for(q, u)
        ss = []
        for (st, n, mask) in srcs:
            kk = kb_ref[pl.ds(st, n), kslab * LANES:(kslab + 1) * LANES]
            sc = _dot_nt(qm, kk)
            if mask is not None:
                sc = jnp.where(mask, sc, -jnp.inf)
            ss.append(sc)
        mx = ss[0].max(axis=-1, keepdims=True)
        for sc in ss[1:]:
            mx = jnp.maximum(mx, sc.max(axis=-1, keepdims=True))
        if sink is not None:
            mx = jnp.maximum(mx, sink)
        return ss, mx

    def softmax_pv(ss, mx, vslab, srcs, sink, upper):
        ol = jnp.zeros((tq, LANES), F32)
        for (st, n, _), sc in zip(srcs, ss):
            pexp = jnp.exp((sc - mx).astype(BF16))
            ol = ol + _dot(pexp, vb_ref[pl.ds(st, n), vslab * LANES:(vslab + 1) * LANES])
        if sink is not None:
            ol = ol + jnp.where(lane128 >= HEAD_DIM, jnp.exp(sink - mx), 0.0)
        sw = pltpu.roll(ol, HEAD_DIM, 1)
        return sw / ol if upper else ol / sw

    if diff:
        lam_init = 0.8 - 0.6 * math.exp(-0.3 * li)
        lv = lam_ref[...]
        lam = (jnp.exp(jnp.sum(lv[0:1] * lv[1:2], axis=1, keepdims=True))
               - jnp.exp(jnp.sum(lv[2:3] * lv[3:4], axis=1, keepdims=True)) + lam_init)

    n_u = 8 if diff else 4
    sink_of = lambda u: sink_ref[li, u] if kind == "B" else None

    def finish(outs, j):
        halves = [outs[2 * h] - lam * outs[2 * h + 1] for h in range(4)] if diff else outs
        low = lane128 < HEAD_DIM
        acc = jnp.concatenate([jnp.where(low, halves[0], halves[1]), jnp.where(low, halves[2], halves[3])], axis=1)
        if diff:
            sq = acc * acc
            inv = jnp.zeros((tq, 256), F32)
            for h in range(4):
                head = (lane256 >= h * HEAD_DIM) & (lane256 < (h + 1) * HEAD_DIM)
                ms = jnp.sum(jnp.where(head, sq, 0.0), axis=-1, keepdims=True) * (1.0 / HEAD_DIM)
                inv = jnp.where(head, lax.rsqrt(ms + NORM_EPS), inv)
            acc = acc * inv * gdo_ref[...] * (1.0 - lam_init)
        y_ref[j * tq:(j + 1) * tq, :] = acc

    def run(jobs, batched):
        qs = [p_ref[pl.ds(row0, tq), 0:256] * scale for (row0, _) in jobs]
        items = [(j, u) for j in range(len(jobs)) for u in range(n_u)]
        ahead = len(items) if batched else 2
        issue = lambda j, u: scores(qs[j], u, jobs[j][1], sink_of(u))
        pending = [issue(*it) for it in items[:ahead]]
        outs = [[] for _ in jobs]
        for k, (j, u) in enumerate(items):
            if k + ahead < len(items):
                pending.append(issue(*items[k + ahead]))
            h = u // 2 if diff else u
            outs[j].append(softmax_pv(*pending.pop(0), h if diff else h // 2, jobs[j][1], sink_of(u), h % 2 == 1))
            if u == n_u - 1:
                finish(outs[j], j)

    base = qi * (n_sub * tq)
    rows0 = [pl.multiple_of(base + j * tq, tq) for j in range(n_sub)]

    @pl.when(is_ctx)
    def _ctx():
        run([(q0, [(q0, tq, None)]) for q0 in rows0], True)

    @pl.when(jnp.logical_not(is_ctx))
    def _lat():
        jobs = []
        for q0 in rows0:
            if kind == "B":
                span = tq + 2 * WINDOW
                st = pl.multiple_of(jnp.clip(q0 - WINDOW, 0, r - span), WINDOW)
                qpos = q0 + lax.broadcasted_iota(I32, (tq, span), 0)
                kpos = st + lax.broadcasted_iota(I32, (tq, span), 1)
                jobs.append((q0, [(st, span, jnp.abs(qpos - kpos) <= WINDOW), (r, past, None)]))
            else:
                jobs.append((q0, [(0, r + past, None)]))
        run(jobs, False)


def _attention(kind, geo, li, p, ck, cv, sink, lam, gdo):
    t_all, w = p.shape
    r, past, tq, ng = geo["r"], geo["past"], geo["tq_block"], geo["n_ctx_groups"]
    ns = t_all // r
    wk = 256 if kind == "D" else 128
    cache_idx = lambda s, q: (jnp.maximum(s - ng, 0), li, 0, 0)
    return pl.pallas_call(
        functools.partial(_attn_kernel, kind, geo, li),
        grid=(ns, r // tq),
        in_specs=[pl.BlockSpec((r, w), lambda s, q: (s, 0)),
                  pl.BlockSpec((1, 1, past, ck.shape[-1]), cache_idx),
                  pl.BlockSpec((1, 1, past, cv.shape[-1]), cache_idx),
                  pl.BlockSpec(memory_space=pltpu.SMEM),
                  _layer_spec(lam, li, 2), _layer_spec(gdo, li, 2)],
        out_specs=pl.BlockSpec((tq, 256), lambda s, q: (s * (r // tq) + q, 0)),
        out_shape=jax.ShapeDtypeStruct((t_all, 256), F32),
        scratch_shapes=[pltpu.VMEM((r + past, wk), BF16), pltpu.VMEM((r + past, 2 * wk), BF16)],
        compiler_params=_params(("arbitrary", "arbitrary")),
    )(p, ck, cv, sink, lam, gdo)


def _ssd_kernel(geo, li, p_ref, st0_ref, cw_ref, cb_ref, dtb_ref, alog_ref, dsk_ref, gs_ref,
                y_ref, so_ref, xpad, u_scr, dt_scr, la_scr, y_scr, upd_scr, cd_scr, ex_scr, st_scr, sf_ref, sb_ref):
    r, seq, ng = geo["r"], geo["seq"], geo["n_ctx_groups"]
    cw_ref, cb_ref, dtb_ref, alog_ref, dsk_ref, gs_ref = (
        t.at[0] for t in (cw_ref, cb_ref, dtb_ref, alog_ref, dsk_ref, gs_ref))
    q = SSM_CHUNK
    nh = SSM_HEADS
    is_ctx = pl.program_id(0) < ng
    inner = nh * SSM_HEAD_DIM
    conv_ch = inner + 2 * LANES
    nchunks = r // q
    spg = r // seq
    gap = 8

    def fill(n_seq, length):
        for j in range(n_seq + 1):
            xpad[j * (length + gap):j * (length + gap) + gap, :] = jnp.zeros((gap, conv_ch), F32)
        for j in range(n_seq):
            xpad[gap + j * (length + gap):gap + j * (length + gap) + length, :] = (
                p_ref[j * length:(j + 1) * length, inner:inner + conv_ch])

    @pl.when(is_ctx)
    def _():
        fill(spg, seq)

    @pl.when(jnp.logical_not(is_ctx))
    def _():
        fill(1, r)

    a_neg = -jnp.exp(alog_ref[...])
    seq_shift = jnp.where(is_ctx, int(math.log2(seq // q)), int(math.log2(r // q)))

    def pre(c, carry):
        r0 = pl.multiple_of(c * q, q)
        st = pl.multiple_of(r0 + lax.shift_right_logical(c, seq_shift) * gap, gap)
        big = xpad[pl.ds(st, q + 16), :]
        acc = jnp.zeros((q, conv_ch), F32) + cb_ref[...]
        for j in range(SSM_CONV):
            sh = pltpu.roll(big, (q + 16) - (gap - SSM_CONV // 2 + j), 0)[0:q]
            acc = acc + sh * cw_ref[j:j + 1, :]
        u_scr[pl.ds(r0, q), :] = _silu(acc)
        raw = p_ref[pl.ds(r0, q), inner + conv_ch:inner + conv_ch + LANES] + dtb_ref[...]
        dt = jnp.maximum(raw, 0.0) + jnp.log1p(jnp.exp(-jnp.abs(raw)))
        dt_scr[pl.ds(r0, q), :] = dt
        la_scr[pl.ds(r0, q), :] = dt * a_neg
        return carry

    lax.fori_loop(0, nchunks, pre, 0)

    row_i = lax.broadcasted_iota(I32, (q, q), 0)
    col_i = lax.broadcasted_iota(I32, (q, q), 1)
    keeps = (col_i <= row_i, col_i >= row_i)
    tris = tuple(jnp.where(k, 1.0, 0.0).astype(BF16) for k in keeps)
    lane128 = lax.broadcasted_iota(I32, (q, LANES), 1)
    lane256 = lax.broadcasted_iota(I32, (q, inner), 1)
    heads256 = [(lane256 >= h * SSM_HEAD_DIM) & (lane256 < (h + 1) * SSM_HEAD_DIM) for h in range(nh)]
    lane8 = lax.broadcasted_iota(I32, (8, inner), 1)
    gmask_t = (lax.broadcasted_iota(I32, (LANES, inner), 0) // SSM_STATE
               == lax.broadcasted_iota(I32, (LANES, inner), 1) // (2 * SSM_HEAD_DIM))

    def phase1(c, carry):
        r0 = pl.multiple_of(c * q, q)
        u = u_scr[pl.ds(r0, q), :]
        xs, bm, cm = u[:, 0:inner], u[:, inner:inner + LANES], u[:, inner + LANES:inner + 2 * LANES]
        bm_b = bm.astype(BF16)
        cb = [_dot_nt(jnp.where((lane128 // SSM_STATE) == g, cm, 0.0).astype(BF16), bm_b) for g in range(2)]
        a1, a2, a3 = _split3(la_scr[pl.ds(r0, q), :])
        cums = [_dot(t, a1) + _dot(t, a2) + _dot(t, a3) for t in tris]
        cum = jnp.where(lane128 < nh, cums[0], cums[1])
        pack = jnp.where(lane128 < 2 * nh, cum, pltpu.roll(dt_scr[pl.ds(r0, q), :], 2 * nh, 1))
        pack_t = pack.T
        bm_t = bm.T
        xstack = jnp.concatenate([jnp.where(heads256[h], xs, 0.0).astype(BF16) for h in range(nh)], axis=0)
        ms, bws = [], []
        for d in range(2):
            edge = q - 1 if d == 0 else 0
            m_h, bw_h, ecol, cdx = [], [], [], jnp.zeros((8, inner), F32)
            for h in range(nh):
                c_ = d * nh + h
                colb = jnp.broadcast_to(cum[:, c_:c_ + 1], (q, q))
                crow = pack_t[c_:c_ + 1, :]
                dtrow = pack_t[2 * nh + c_:2 * nh + c_ + 1, :]
                lm = jnp.exp(jnp.where(keeps[d], colb - crow, -jnp.inf))
                m_h.append((cb[h // 2] * lm * dtrow).astype(BF16))
                tot = crow[:, edge:edge + 1]
                bw_h.append((bm_t * (dtrow * jnp.exp(tot - crow))).astype(BF16))
                ecol.append(jnp.exp(colb))
                cdx = jnp.where((lane8 >= h * SSM_HEAD_DIM) & (lane8 < (h + 1) * SSM_HEAD_DIM), jnp.exp(tot), cdx)
            ms.append(jnp.concatenate(m_h, axis=1))
            bws.append(jnp.concatenate(bw_h, axis=1))
            ex_scr[d, pl.ds(r0, q), :] = jnp.concatenate(
                [jnp.where(lane128 < SSM_HEAD_DIM, ecol[0], ecol[1]),
                 jnp.where(lane128 < SSM_HEAD_DIM, ecol[2], ecol[3])], axis=1)
            cd_scr[d, c] = cdx
        res = _dot(jnp.concatenate(ms + bws, axis=0), xstack)
        y_scr[pl.ds(r0, q), :] = res[0:q] + res[q:2 * q]
        upd_scr[0, c] = jnp.where(gmask_t, res[2 * q:3 * q], 0.0)
        upd_scr[1, c] = jnp.where(gmask_t, res[3 * q:4 * q], 0.0)
        return carry

    lax.fori_loop(0, nchunks, phase1, 0, unroll=2)

    srefs = (sf_ref, sb_ref)

    def scan_step(d, c):
        s_in = srefs[d][...]
        st_scr[d, c] = s_in.astype(BF16)
        srefs[d][...] = s_in * cd_scr[d, c][0:1, :] + upd_scr[d, c]

    @pl.when(is_ctx)
    def _ctx():
        cps = seq // q

        def one_seq(j, carry):
            for d in range(2):
                srefs[d][...] = jnp.zeros((LANES, inner), F32)
                for ci in range(cps):
                    scan_step(d, j * cps + (ci if d == 0 else cps - 1 - ci))
                sv = srefs[d][...]
                so_ref[j, d, 0:LANES, :] = sv[:, 0:LANES].T[:, 0:SSM_STATE]
                so_ref[j, d, LANES:inner, :] = pltpu.roll(sv[:, LANES:inner].T, SSM_STATE, 1)[:, 0:SSM_STATE]
            return carry

        lax.fori_loop(0, spg, one_seq, 0)

    @pl.when(jnp.logical_not(is_ctx))
    def _lat():
        for d in range(2):
            srefs[d][...] = st0_ref[0, 0, d]

        def body(i, carry):
            scan_step(0, i)
            scan_step(1, nchunks - 1 - i)
            return carry

        lax.fori_loop(0, nchunks, body, 0)

    def fin(c, carry):
        r0 = pl.multiple_of(c * q, q)
        cm_b = u_scr[pl.ds(r0, q), inner + LANES:inner + 2 * LANES].astype(BF16)
        y = y_scr[pl.ds(r0, q), :] + u_scr[pl.ds(r0, q), 0:inner] * dsk_ref[...]
        for d in range(2):
            y = y + _dot(cm_b, st_scr[d, c]) * ex_scr[d, pl.ds(r0, q), :]
        y = y * _silu(p_ref[pl.ds(r0, q), 0:inner])
        y_ref[pl.ds(r0, q), :] = _rms(y, gs_ref[...])
        return carry

    lax.fori_loop(0, nchunks, fin, 0, unroll=4)


def _ssd(geo, li, pc, st0, cw, cb, dtb, alog, dsk, gs):
    t_all, w = pc.shape
    r, seq, ng = geo["r"], geo["seq"], geo["n_ctx_groups"]
    ns = t_all // r
    spg = r // seq
    nchunks = r // SSM_CHUNK
    inner = SSM_HEADS * SSM_HEAD_DIM
    full = lambda a: _layer_spec(a, li, 1)
    return pl.pallas_call(
        functools.partial(_ssd_kernel, geo, li),
        grid=(ns,),
        in_specs=[pl.BlockSpec((r, w), lambda s: (s, 0)),
                  pl.BlockSpec((1, 1, 2, LANES, inner), lambda s: (jnp.maximum(s - ng, 0), li, 0, 0, 0)),
                  full(cw), full(cb), full(dtb), full(alog), full(dsk), full(gs)],
        out_specs=[pl.BlockSpec((r, inner), lambda s: (s, 0)),
                   pl.BlockSpec((spg, 2, inner, SSM_STATE), lambda s: (jnp.minimum(s, ng - 1), 0, 0, 0))],
        out_shape=[jax.ShapeDtypeStruct((t_all, inner), F32),
                   jax.ShapeDtypeStruct((ng * spg, 2, inner, SSM_STATE), F32)],
        scratch_shapes=[pltpu.VMEM((r + 8 * (spg + 1), 2 * inner), F32), pltpu.VMEM((r, 2 * inner), F32),
                        pltpu.VMEM((r, LANES), F32), pltpu.VMEM((r, LANES), F32),
                        pltpu.VMEM((r, inner), F32),
                        pltpu.VMEM((2, nchunks, LANES, inner), F32), pltpu.VMEM((2, nchunks, 8, inner), F32),
                        pltpu.VMEM((2, r, inner), F32), pltpu.VMEM((2, nchunks, LANES, inner), BF16),
                        pltpu.VMEM((LANES, inner), F32), pltpu.VMEM((LANES, inner), F32)],
        compiler_params=_params(("arbitrary",)),
    )(pc, st0, cw, cb, dtb, alog, dsk, gs)


def _merge_kernel(nct, xc_ref, xl_ref, mod_ref, g1_ref, g2_ref, ya_ref, yb_ref, yc_ref, yd_ref, wg_ref, wbr_ref,
                  wo_ref, wr_ref, x1_ref, h2_ref, aff_ref):
    g1_ref, g2_ref, wg_ref, wbr_ref, wo_ref, wr_ref = (
        t.at[0] for t in (g1_ref, g2_ref, wg_ref, wbr_ref, wo_ref, wr_ref))
    tm, d = xc_ref.shape
    is_ctx = pl.program_id(0) < nct
    m = mod_ref[0]
    w_hi, w_lo = _split2(wr_ref[...])
    n_sub = 2
    ts = tm // n_sub

    def mix(i):
        rows = slice(i * ts, (i + 1) * ts)
        x = jnp.where(is_ctx, xc_ref[rows, :], xl_ref[rows, :])
        hb = (_rms(x, g1_ref[...]) * (1.0 + m[1:2]) + m[0:1]).astype(BF16)
        merged = jnp.zeros(x.shape, F32)
        for n, y_ref in enumerate((ya_ref, yb_ref, yc_ref, yd_ref)):
            gate = jax.nn.sigmoid(_dot(hb, wg_ref[:, n * d:(n + 1) * d]))
            merged = merged + gate * _dot(y_ref[rows, :].astype(BF16), wbr_ref[n])
        return x + m[2:3] * _dot(merged.astype(BF16), wo_ref[...])

    def tail(i, x1):
        rows = slice(i * ts, (i + 1) * ts)
        x1_ref[rows, :] = x1
        h2 = _rms(x1, g2_ref[...]) * (1.0 + m[4:5]) + m[3:4]
        h2_ref[rows, :] = h2.astype(BF16)
        h_hi, h_lo = _split2(h2)
        logit = _dot_nt(w_hi, h_hi) + _dot_nt(w_hi, h_lo) + _dot_nt(w_lo, h_hi)
        e = jnp.exp(logit - logit.max(axis=0, keepdims=True))
        aff_ref[:, rows] = e / e.sum(axis=0, keepdims=True)

    x1_prev = mix(0)
    for i in range(1, n_sub):
        x1_next = mix(i)
        tail(i - 1, x1_prev)
        x1_prev = x1_next
    tail(n_sub - 1, x1_prev)


def _merge(x, mod, g1, g2, ys, wg, wbr, wo, wr_t, geo, tm, li):
    d = x[0].shape[1]
    t_all = x[0].shape[0] + x[1].shape[0]
    ne = wr_t.shape[1]
    nct = geo["t_ctx"] // tm
    tps = geo["r"] // tm
    mod_row = lambda i: li * MOD_ROWS + jnp.where(i < nct, 0, 1 + (i - nct) // tps)
    full = lambda a: _layer_spec(a, li, 1)
    ytile = pl.BlockSpec((tm, BRANCH_W), lambda i: (i, 0))
    return pl.pallas_call(
        functools.partial(_merge_kernel, nct),
        grid=(t_all // tm,),
        in_specs=[pl.BlockSpec((tm, d), lambda i: (jnp.minimum(i, nct - 1), 0)),
                  pl.BlockSpec((tm, d), lambda i: (jnp.maximum(i - nct, 0), 0)),
                  pl.BlockSpec((1, 6, d), lambda i: (mod_row(i), 0, 0)),
                  full(g1), full(g2), ytile, ytile, ytile, ytile, full(wg), full(wbr), full(wo), full(wr_t)],
        out_specs=[pl.BlockSpec((tm, d), lambda i: (i, 0)),
                   pl.BlockSpec((tm, d), lambda i: (i, 0)),
                   pl.BlockSpec((ne, tm), lambda i: (0, i))],
        out_shape=[jax.ShapeDtypeStruct((t_all, d), F32),
                   jax.ShapeDtypeStruct((t_all, d), BF16),
                   jax.ShapeDtypeStruct((ne, t_all), F32)],
        compiler_params=_params(("arbitrary",)),
    )(x[0], x[1], mod, g1, g2, *ys, wg, wbr, wo, wr_t)


def _kth_largest_bits(a, cap):
    def keeps(cand):
        cnt = jnp.sum(jnp.where(a >= lax.bitcast_convert_type(cand, F32), 1.0, 0.0), axis=1, keepdims=True)
        return cnt >= cap

    def body(i, thr):
        lo = 29 - 2 * i
        c1, c2, c3 = (thr | lax.shift_left(jnp.int32(k), lo) for k in (1, 2, 3))
        return jnp.where(keeps(c3), c3, jnp.where(keeps(c2), c2, jnp.where(keeps(c1), c1, thr)))

    thr = lax.fori_loop(0, 15, body, jnp.zeros((a.shape[0], 1), I32))
    return jnp.where(keeps(thr | 1), thr | 1, thr)


def _route_kernel(geo, aff_ref, slot_ref, start_ref):
    r, seq, ng = geo["r"], geo["seq"], geo["n_ctx_groups"]
    ne = aff_ref.shape[0]
    blk = ROUTE_TILE
    s = pl.program_id(0)
    lane_s = lax.broadcasted_iota(I32, (ne, LANES), 1)
    upper = jnp.where(lax.broadcasted_iota(I32, (blk, blk), 0) < lax.broadcasted_iota(I32, (blk, blk), 1),
                      1.0, 0.0).astype(BF16)

    def select(a, cap, nblk):
        thr = _kth_largest_bits(a, cap)
        gt = a >= lax.bitcast_convert_type(thr + 1, F32)
        eq = (a >= lax.bitcast_convert_type(thr, F32)) & jnp.logical_not(gt)
        need = cap - jnp.sum(jnp.where(gt, 1.0, 0.0), axis=1, keepdims=True)
        eq_f = jnp.where(eq, 1.0, 0.0)
        outs, firsts = [], []
        carry_e = jnp.zeros((a.shape[0], 1), F32)
        carry_s = jnp.zeros((a.shape[0], 1), F32)
        for b in range(nblk):
            sl = slice(b * blk, (b + 1) * blk)
            pe = _dot(eq_f[:, sl].astype(BF16), upper) + carry_e
            sel = gt[:, sl] | (eq[:, sl] & (pe < need))
            sel_f = jnp.where(sel, 1.0, 0.0)
            ps = _dot(sel_f.astype(BF16), upper) + carry_s
            outs.append(jnp.where(sel, ps, -1.0))
            firsts.append(carry_s)
            carry_e = carry_e + eq_f[:, sl].sum(axis=1, keepdims=True)
            carry_s = carry_s + sel_f.sum(axis=1, keepdims=True)
        return outs, firsts

    @pl.when(s < ng)
    def _ctx():
        spg = r // seq
        cap = EC_FACTOR * seq // ne
        a = jnp.concatenate([aff_ref[:, j * seq:(j + 1) * seq] for j in range(spg)], axis=0)
        out, _ = select(a, cap, 1)
        for j in range(spg):
            o = out[0][j * ne:(j + 1) * ne]
            slot_ref[:, j * seq:(j + 1) * seq] = jnp.where(o >= 0, o + j * cap, -1.0).astype(I32)
        start_ref[...] = jnp.minimum(lane_s, spg) * cap

    @pl.when(s >= ng)
    def _lat():
        cap = EC_FACTOR * r // ne
        out, firsts = select(aff_ref[...], cap, r // blk)
        st = jnp.full((ne, LANES), float(cap), F32)
        for b, o in enumerate(out):
            slot_ref[:, b * blk:(b + 1) * blk] = o.astype(I32)
            st = jnp.where(lane_s == b, firsts[b], st)
        start_ref[...] = st.astype(I32)


def _route(aff_t, geo):
    ne, t_all = aff_t.shape
    r = geo["r"]
    return pl.pallas_call(
        functools.partial(_route_kernel, geo),
        grid=(t_all // r,),
        in_specs=[pl.BlockSpec((ne, r), lambda s: (0, s))],
        out_specs=[pl.BlockSpec((ne, r), lambda s: (0, s)), pl.BlockSpec((ne, LANES), lambda s: (0, s))],
        out_shape=[jax.ShapeDtypeStruct((ne, t_all), I32), jax.ShapeDtypeStruct((ne, (t_all // r) * LANES), I32)],
        compiler_params=_params(("arbitrary",)),
    )(aff_t)


def _tile_starts(starts_ref, e, s, ns, nt):
    base = (e * ns + s) * (nt + 1)
    return [starts_ref[base + i] for i in range(nt + 1)]


def _expert_kernel(starts_ref, x_ref, slot_ref, aff_ref, w1_ref, w3_ref, w2_ref, yo_ref, wb1, wb3, wb2,
                   xs_scr, gs_scr):
    e, s, ns = pl.program_id(0), pl.program_id(1), pl.num_programs(1)

    @pl.when(s == 0)
    def _cast():
        wb1[...] = w1_ref[0, 0].astype(BF16)
        wb3[...] = w3_ref[0, 0].astype(BF16)
        wb2[...] = w2_ref[0, 0].astype(BF16)

    cap = yo_ref.shape[2]
    r = x_ref.shape[0]
    nt = r // ROUTE_TILE
    win = min(GATHER_WINDOW, cap)
    slot = slot_ref[0]
    aff = aff_ref[0]
    a = _tile_starts(starts_ref, e, s, ns, nt)
    fits = a[1] - a[0] <= win - 16
    for i in range(1, nt):
        fits = fits & (a[i + 1] - a[i] <= win - 16)

    @pl.when(fits)
    def _windows():
        xs_scr[...] = jnp.zeros(xs_scr.shape, F32)
        gs_scr[...] = jnp.zeros(gs_scr.shape, F32)
        for i in range(nt):
            cols = slice(i * ROUTE_TILE, (i + 1) * ROUTE_TILE)
            w0 = pl.multiple_of(jnp.minimum((a[i] // 16) * 16, cap - win), 16)
            pick = (lax.broadcasted_iota(I32, (win, ROUTE_TILE), 0) + w0) == slot[:, cols]
            xs_scr[pl.ds(w0, win), :] += _dot(jnp.where(pick, 1.0, 0.0).astype(BF16), x_ref[cols, :])
            gs_scr[pl.ds(w0, win), :] += jnp.sum(jnp.where(pick, aff[:, cols], 0.0), axis=1, keepdims=True)

    @pl.when(jnp.logical_not(fits))
    def _all_rows():
        pick = lax.broadcasted_iota(I32, (cap, r), 0) == slot
        xs_scr[...] = _dot(jnp.where(pick, 1.0, 0.0).astype(BF16), x_ref[...])
        gs_scr[...] = jnp.broadcast_to(jnp.sum(jnp.where(pick, aff, 0.0), axis=1, keepdims=True), gs_scr.shape)

    xs = xs_scr[...].astype(BF16)
    hid = _silu(_dot(xs, wb1[...])) * _dot(xs, wb3[...])
    yo_ref[0, 0] = (_dot(hid.astype(BF16), wb2[...]) * gs_scr[:, 0:1]).astype(BF16)


def _experts(starts, h2, slot3, aff3, w1, w3, w2, geo, li):
    t_all, d = h2.shape
    _, ne, _, f = w1.shape
    r = geo["r"]
    ns = t_all // r
    cap = EC_FACTOR * r // ne
    wspec = lambda a: pl.BlockSpec((1, 1) + a.shape[2:], lambda e, s, st: (li, e, 0, 0))
    return pl.pallas_call(
        _expert_kernel,
        grid_spec=pltpu.PrefetchScalarGridSpec(
            num_scalar_prefetch=1,
            grid=(ne, ns),
            in_specs=[pl.BlockSpec((r, d), lambda e, s, st: (s, 0)),
                      pl.BlockSpec((1, 1, r), lambda e, s, st: (e, 0, s)),
                      pl.BlockSpec((1, 1, r), lambda e, s, st: (e, 0, s)),
                      wspec(w1), wspec(w3), wspec(w2)],
            out_specs=pl.BlockSpec((1, 1, cap, d), lambda e, s, st: (s, e, 0, 0)),
            scratch_shapes=[pltpu.VMEM((d, f), BF16), pltpu.VMEM((d, f), BF16), pltpu.VMEM((f, d), BF16),
                            pltpu.VMEM((cap, d), F32), pltpu.VMEM((cap, LANES), F32)]),
        out_shape=jax.ShapeDtypeStruct((ns, ne, cap, d), BF16),
        compiler_params=_params(("arbitrary", "arbitrary")),
    )(starts, h2, slot3, aff3, w1, w3, w2)


def _scatter_kernel(ng, starts_ref, x_ref, mod_ref, slot_ref, yo_ref, oc_ref, ol_ref, ywin):
    tt = x_ref.shape[0]
    _, ne, cap, d = yo_ref.shape
    s, i, ns, nt = pl.program_id(0), pl.program_id(1), pl.num_programs(0), pl.num_programs(1)
    win = min(SCATTER_WINDOW, cap)
    slot = slot_ref[...]
    first = [starts_ref[(e * ns + s) * (nt + 1) + i] for e in range(ne)]
    last = [starts_ref[(e * ns + s) * (nt + 1) + i + 1] for e in range(ne)]
    fits = last[0] - first[0] <= win - 16
    for e in range(1, ne):
        fits = fits & (last[e] - first[e] <= win - 16)

    is_ctx = s < ng

    def emit(ffn):
        out = x_ref[...] + mod_ref[0][5:6] * ffn

        @pl.when(is_ctx)
        def _():
            oc_ref[...] = out

        @pl.when(jnp.logical_not(is_ctx))
        def _():
            ol_ref[...] = out

    @pl.when(fits)
    def _windows():
        lane = lax.broadcasted_iota(I32, (tt, win), 1)
        hot = []
        for e in range(ne):
            w0 = pl.multiple_of(jnp.minimum((first[e] // 16) * 16, cap - win), 16)
            ywin[e * win:(e + 1) * win, :] = yo_ref[0, e, pl.ds(w0, win), :]
            hot.append(jnp.where(slot[:, e:e + 1] - w0 == lane, 1.0, 0.0).astype(BF16))
        emit(_dot(jnp.concatenate(hot, axis=1), ywin[...]))

    @pl.when(jnp.logical_not(fits))
    def _all_rows():
        lane = lax.broadcasted_iota(I32, (tt, cap), 1)
        hot = jnp.concatenate([jnp.where(slot[:, e:e + 1] == lane, 1.0, 0.0).astype(BF16) for e in range(ne)], axis=1)
        emit(_dot(hot, yo_ref[0].reshape(ne * cap, d)))


def _scatter(starts, x1, mod, slot_t, yo, geo, li):
    t_all, d = x1.shape
    ns, ne, cap, _ = yo.shape
    r = geo["r"]
    ng = geo["n_ctx_groups"]
    tt = ROUTE_TILE
    tpg = r // tt
    nct = ng * tpg
    win = min(SCATTER_WINDOW, cap)
    mod_row = lambda s, i, st: li * MOD_ROWS + jnp.where(s < ng, 0, 1 + s - ng)
    out_specs = [pl.BlockSpec((tt, d), lambda s, i, st: (jnp.minimum(s * tpg + i, nct - 1), 0)),
                 pl.BlockSpec((tt, d), lambda s, i, st: (jnp.maximum(s * tpg + i - nct, 0), 0))]
    out_shape = [jax.ShapeDtypeStruct((nct * tt, d), F32), jax.ShapeDtypeStruct((t_all - nct * tt, d), F32)]
    return pl.pallas_call(
        functools.partial(_scatter_kernel, ng),
        grid_spec=pltpu.PrefetchScalarGridSpec(
            num_scalar_prefetch=1,
            grid=(ns, tpg),
            in_specs=[pl.BlockSpec((tt, d), lambda s, i, st: (s * tpg + i, 0)),
                      pl.BlockSpec((1, 6, d), lambda s, i, st: (mod_row(s, i, st), 0, 0)),
                      pl.BlockSpec((tt, ne), lambda s, i, st: (s * tpg + i, 0)),
                      pl.BlockSpec((1, ne, cap, d), lambda s, i, st: (s, 0, 0, 0))],
            out_specs=out_specs,
            scratch_shapes=[pltpu.VMEM((ne * win, d), BF16)]),
        out_shape=out_shape,
        compiler_params=_params(("arbitrary", "arbitrary")),
    )(starts, x1, mod, slot_t, yo)


def _rope_table(s, d, n_qk_lanes, n_v_lanes, pad_rows):
    n = d // 4
    t = np.arange(s)
    rows = (t // GRID_W).astype(np.float64)
    cols = (t % GRID_W).astype(np.float64)
    inv = ROPE_THETA ** (-np.arange(n, dtype=np.float64) / n)
    ang = np.stack([rows[:, None] * inv, cols[:, None] * inv], axis=1)
    cos, sin = np.cos(ang), np.sin(ang)
    cos_h = np.concatenate([cos[:, 0], cos[:, 0], cos[:, 1], cos[:, 1]], axis=-1)
    sin_h = np.concatenate([-sin[:, 0], sin[:, 0], -sin[:, 1], sin[:, 1]], axis=-1)
    reps = n_qk_lanes // d
    cos_f = np.concatenate([np.tile(cos_h, (1, reps)), np.ones((s, n_v_lanes))], axis=1)
    sin_f = np.concatenate([np.tile(sin_h, (1, reps)), np.zeros((s, n_v_lanes))], axis=1)
    w = n_qk_lanes + n_v_lanes
    cos_f = np.concatenate([cos_f, np.ones((pad_rows, w))], axis=0)
    sin_f = np.concatenate([sin_f, np.zeros((pad_rows, w))], axis=0)
    return jnp.asarray(cos_f.astype(np.float32)), jnp.asarray(sin_f.astype(np.float32))


def _block_diag_ones(w, gsize):
    i = np.arange(w)
    return jnp.asarray((i[:, None] // gsize == i[None, :] // gsize).astype(np.float32), dtype=BF16)


def kernel(x_prompt, x_sample, c, cache_a_k, cache_a_v, cache_b_k, cache_b_v, state_ssm, cache_d_k, cache_d_v, c_ctx, w_ada, b_ada, g_norm1, g_norm2, w_in, g_qa, g_ka, g_qb, g_kb, sink_b, conv_w, conv_b, dt_bias, a_log, d_skip, g_ssm, g_qd, g_kd, lam_q1, lam_k1, lam_q2, lam_k2, g_dout, w_br, w_out, w_router, w_e1, w_e3, w_e2):
    batch, seq, d = x_prompt.shape
    dec_batch, dec_seq, _ = x_sample.shape
    depth = w_in.shape[0]
    past = cache_a_k.shape[2]
    ne = w_router.shape[2]
    r = dec_seq
    t_ctx = batch * seq
    assert t_ctx % r == 0 and r % seq == 0 and seq == 256 and past % 16 == 0
    tm = 512
    assert t_ctx % tm == 0 and r % tm == 0
    geo = dict(r=r, seq=seq, past=past, tq=256, tq_block=512, t_ctx=t_ctx, n_ctx_groups=t_ctx // r)
    ng = geo["n_ctx_groups"]

    cosa, sina = _rope_table(r, HEAD_DIM, 384, 128, tm)
    cosd, sind = _rope_table(r, DIFF_QK_DIM, 512, 256, tm)
    bda = _block_diag_ones(256, HEAD_DIM)
    bdd = _block_diag_ones(256, DIFF_QK_DIM)
    inner = SSM_HEADS * SSM_HEAD_DIM
    gsel = (np.arange(LANES)[:, None] // SSM_STATE) == (np.arange(inner)[None, :] // (2 * SSM_HEAD_DIM))

    cvec = jnp.concatenate([c_ctx[None, :], c, jnp.zeros((MOD_ROWS - 1 - dec_batch, d), F32)], axis=0)
    mod_all = _adaln(cvec, w_ada, b_ada)

    x = (x_prompt.reshape(t_ctx, d), x_sample.reshape(dec_batch * r, d))
    ck_a = cache_a_k.reshape(dec_batch, depth, past, -1)
    cv_a = cache_a_v.reshape(dec_batch, depth, past, -1)
    ck_b = cache_b_k.reshape(dec_batch, depth, past, -1)
    cv_b = cache_b_v.reshape(dec_batch, depth, past, -1)
    ck_d = cache_d_k.reshape(dec_batch, depth, past, -1)
    cv_d = cache_d_v.reshape(dec_batch, depth, past, -1)
    st0 = jnp.moveaxis(state_ssm, -1, -3).reshape(dec_batch, depth, 2, SSM_STATE, inner)
    st0 = jnp.where(gsel, jnp.concatenate([st0, st0], axis=-2), 0.0)

    mod = mod_all.reshape(depth * MOD_ROWS, 6, d)
    w_mix = jnp.concatenate([w_in[:, :, :DT_SRC], jnp.zeros((depth, d, COL_C[1] - COL_C[0] - 776), F32),
                             w_in[:, :, DT_SRC:N_MIX_SRC]], axis=2).astype(BF16)
    w_gate = w_in[:, :, N_MIX_SRC:].astype(BF16)
    w_br_b, w_out_b = w_br.astype(BF16), w_out.astype(BF16)
    w_r_t = jnp.swapaxes(w_router, 1, 2)
    lay = lambda *parts: jnp.concatenate(parts, axis=-1)[:, None, :]
    rep = lambda g, n: jnp.tile(g, (1, n))
    consts = dict(
        bda=bda, bdd=bdd, cosa=cosa, sina=sina, cosd=cosd, sind=sind,
        ga=lay(rep(g_qa, 4), rep(g_ka, 2), jnp.ones((depth, 128), F32)),
        gb=lay(rep(g_qb, 4), rep(g_kb, 2), jnp.ones((depth, 128), F32)),
        gd=lay(rep(g_qd, 8), rep(g_kd, 8), jnp.ones((depth, 256), F32)))
    g1, g2 = g_norm1[:, None, :], g_norm2[:, None, :]
    lam = jnp.stack([lam_q1, lam_k1, lam_q2, lam_k2], axis=1)
    gdo = rep(g_dout, 4)[:, None, :]
    pad3 = lambda a, rows, lanes: jnp.pad(a, ((0, 0), (0, rows - a.shape[1]), (0, lanes - a.shape[2])))
    ssd_par = (pad3(conv_w, 8, conv_w.shape[2]), conv_b[:, None, :],
               pad3(dt_bias.reshape(depth, 1, -1), 1, LANES), pad3(a_log.reshape(depth, 1, -1), 1, LANES),
               jnp.repeat(d_skip, SSM_HEAD_DIM, axis=1)[:, None, :], g_ssm[:, None, :])

    new = {k: [] for k in ("a_k", "a_v", "b_k", "b_v", "ssm", "d_k", "d_v")}
    for li in range(depth):
        pa, pb, pc, pd = _inproj(x, mod, g1, w_mix, consts, geo, tm, li)
        ya = _attention("A", geo, li, pa, ck_a, cv_a, sink_b, lam, gdo)
        yb = _attention("B", geo, li, pb, ck_b, cv_b, sink_b, lam, gdo)
        yd = _attention("D", geo, li, pd, ck_d, cv_d, sink_b, lam, gdo)
        yc, ssm_new = _ssd(geo, li, pc, st0, *ssd_par)

        x1, h2, aff_t = _merge(x, mod, g1, g2, (ya, yb, yc, yd), w_gate, w_br_b, w_out_b, w_r_t, geo, tm, li)
        slot, starts = _route(aff_t, geo)
        starts = starts.reshape(ne, -1, LANES)[:, :, :r // ROUTE_TILE + 1].reshape(-1)
        yo = _experts(starts, h2, slot.reshape(ne, 1, -1), aff_t.reshape(ne, 1, -1), w_e1, w_e3, w_e2, geo, li)
        x = _scatter(starts, x1, mod, slot.T, yo, geo, li)

        ctx = lambda p, lo, hi: p[:t_ctx, lo:hi]
        new["a_k"].append(ctx(pa, 256, 384).reshape(batch, seq, 2, HEAD_DIM))
        new["a_v"].append(ctx(pa, 384, 512).reshape(batch, seq, 2, HEAD_DIM))
        new["b_k"].append(ctx(pb, 256, 384).reshape(batch, seq, 2, HEAD_DIM))
        new["b_v"].append(ctx(pb, 384, 512).reshape(batch, seq, 2, HEAD_DIM))
        new["d_k"].append(ctx(pd, 256, 512).reshape(batch, seq, 4, 2, DIFF_QK_DIM))
        new["d_v"].append(ctx(pd, 512, 768).reshape(batch, seq, 4, HEAD_DIM))
        new["ssm"].append(ssm_new.reshape(batch, 2, SSM_HEADS, SSM_HEAD_DIM, SSM_STATE))

    st = lambda k: jnp.stack(new[k], axis=1)
    return (x[0].reshape(batch, seq, d), x[1].reshape(dec_batch, r, d),
            st("a_k"), st("a_v"), st("b_k"), st("b_v"), st("ssm"), st("d_k"), st("d_v"))
```

```python
import functools
import math

import numpy as np
import jax
import jax.numpy as jnp
from jax import lax
from jax.experimental import pallas as pl
from jax.experimental.pallas import tpu as pltpu

F32 = jnp.float32
BF16 = jnp.bfloat16
I32 = jnp.int32

HEAD_DIM = 64
DIFF_QK_DIM = 32
GRID_W = 64
ROPE_THETA = 10000.0
NORM_EPS = 1e-6
WINDOW = 128
SSM_CHUNK = 128
SSM_HEADS = 4
SSM_HEAD_DIM = 64
SSM_STATE = 64
SSM_CONV = 5
EC_FACTOR = 2
BRANCH_W = 256
N_BRANCH = 4

ROUTE_TILE = 256
GATHER_WINDOW = 80
SCATTER_WINDOW = 128

MOD_ROWS = 16
LANES = 128
VMEM_LIMIT = 56 * 1024 * 1024

COL_A = (0, 512)
COL_B = (512, 1024)
COL_C = (1024, 1920)
COL_D = (1920, 2688)
W_MIX = 2688
N_MIX_SRC = 2568
DT_SRC = 1800


def _dot(a, b):
    return jnp.dot(a, b, preferred_element_type=F32)


def _dot_nt(a, b):
    return lax.dot_general(a, b, (((1,), (1,)), ((), ())), preferred_element_type=F32)


def _dot_tn(a, b):
    return lax.dot_general(a, b, (((0,), (0,)), ((), ())), preferred_element_type=F32)


def _split2(x):
    hi = x.astype(BF16)
    lo = (x - hi.astype(F32)).astype(BF16)
    return hi, lo


def _split3(x):
    hi = x.astype(BF16)
    r = x - hi.astype(F32)
    mid = r.astype(BF16)
    lo = (r - mid.astype(F32)).astype(BF16)
    return hi, mid, lo


def _silu(x):
    return x * jax.nn.sigmoid(x)


def _rms(x, g):
    ms = jnp.mean(x * x, axis=-1, keepdims=True)
    return x * lax.rsqrt(ms + NORM_EPS) * g


def _params(sem):
    return pltpu.CompilerParams(dimension_semantics=sem, vmem_limit_bytes=VMEM_LIMIT)


def _adaln_kernel(c_ref, w_ref, b_ref, o_ref):
    s_hi, s_lo = _split2(_silu(c_ref[...]))
    w_hi, w_lo = _split2(w_ref[0])
    o_ref[0] = _dot(s_hi, w_hi) + _dot(s_hi, w_lo) + _dot(s_lo, w_hi) + b_ref[0]


def _split_w_in_kernel(w_ref, mix_ref, gate_ref):
    w = w_ref[0]
    pad = jnp.zeros((w.shape[0], W_MIX - N_MIX_SRC), F32)
    mix_ref[0] = jnp.concatenate([w[:, :DT_SRC], pad, w[:, DT_SRC:N_MIX_SRC]], axis=1).astype(BF16)
    gate_ref[0] = w[:, N_MIX_SRC:].astype(BF16)


def _split_w_in(w_in, rows=128):
    depth, d, n_in = w_in.shape
    return pl.pallas_call(
        _split_w_in_kernel,
        grid=(depth, d // rows),
        in_specs=[pl.BlockSpec((1, rows, n_in), lambda l, i: (l, i, 0))],
        out_specs=[pl.BlockSpec((1, rows, W_MIX), lambda l, i: (l, i, 0)),
                   pl.BlockSpec((1, rows, n_in - N_MIX_SRC), lambda l, i: (l, i, 0))],
        out_shape=[jax.ShapeDtypeStruct((depth, d, W_MIX), BF16),
                   jax.ShapeDtypeStruct((depth, d, n_in - N_MIX_SRC), BF16)],
        compiler_params=_params(("arbitrary", "arbitrary")),
    )(w_in)


def _adaln(cvec, w_ada, b_ada):
    depth, d, n = w_ada.shape
    m = cvec.shape[0]
    tn = 1024
    return pl.pallas_call(
        _adaln_kernel,
        grid=(depth, n // tn),
        in_specs=[pl.BlockSpec((m, d), lambda l, j: (0, 0)),
                  pl.BlockSpec((1, d, tn), lambda l, j: (l, 0, j)),
                  pl.BlockSpec((1, 1, tn), lambda l, j: (l, 0, j))],
        out_specs=pl.BlockSpec((1, m, tn), lambda l, j: (l, 0, j)),
        out_shape=jax.ShapeDtypeStruct((depth, m, n), F32),
        compiler_params=_params(("arbitrary", "arbitrary")),
    )(cvec, w_ada, b_ada.reshape(depth, 1, n))


def _qknorm_rope_store(p, bd, gain, cos_ref, sin_ref, segs, gsize, n, o_ref):
    parts = []
    done = 0
    for (lo, hi) in segs:
        ps = p[:, lo:hi]
        ssum = _dot((ps * ps).astype(BF16), bd[0:hi - lo, 0:hi - lo])
        parts.append(ps * lax.rsqrt(ssum * (1.0 / gsize) + NORM_EPS) * gain[:, lo:hi])
        done = hi
    parts.append(p[:, done:])
    y = jnp.concatenate(parts, axis=1)
    for c in range(p.shape[1] // LANES):
        sl = slice(c * LANES, (c + 1) * LANES)
        ys = y[:, sl]
        up = pltpu.roll(ys, LANES - n, 1)
        dn = pltpu.roll(ys, n, 1)
        l128 = lax.broadcasted_iota(I32, ys.shape, 1)
        sw = jnp.where((l128 & (2 * n - 1)) < n, up, dn)
        o_ref[:, sl] = ys * cos_ref[:, sl] + sw * sin_ref[:, sl]


def _pick_x(nct, xc_ref, xl_ref):
    return jnp.where(pl.program_id(0) < nct, xc_ref[...], xl_ref[...])


def _inproj_kernel(nct, xc_ref, xl_ref, mod_ref, g1_ref, w_ref, bda_ref, bdd_ref, ga_ref, gb_ref, gd_ref,
                   cosa_ref, sina_ref, cosd_ref, sind_ref, pa_ref, pb_ref, pc_ref, pd_ref):
    g1_ref, w_ref, ga_ref, gb_ref, gd_ref = (t.at[0] for t in (g1_ref, w_ref, ga_ref, gb_ref, gd_ref))
    m = mod_ref[0]
    h = _rms(_pick_x(nct, xc_ref, xl_ref), g1_ref[...]) * (1.0 + m[1:2]) + m[0:1]
    hb = h.astype(BF16)
    pa = _dot(hb, w_ref[:, COL_A[0]:COL_A[1]])
    pb = _dot(hb, w_ref[:, COL_B[0]:COL_B[1]])
    _qknorm_rope_store(pa, bda_ref[...], ga_ref[...], cosa_ref, sina_ref, ((0, 256), (256, 384)), HEAD_DIM, HEAD_DIM // 4, pa_ref)
    pd = _dot(hb, w_ref[:, COL_D[0]:COL_D[1]])
    _qknorm_rope_store(pb, bda_ref[...], gb_ref[...], cosa_ref, sina_ref, ((0, 256), (256, 384)), HEAD_DIM, HEAD_DIM // 4, pb_ref)
    pc_ref[...] = _dot(hb, w_ref[:, COL_C[0]:COL_C[1]])
    _qknorm_rope_store(pd, bdd_ref[...], gd_ref[...], cosd_ref, sind_ref, ((0, 256), (256, 512)), DIFF_QK_DIM, DIFF_QK_DIM // 4, pd_ref)


def _layer_spec(a, li, n_grid):
    zeros = (0,) * (a.ndim - 1)
    if n_grid == 1:
        return pl.BlockSpec((1,) + a.shape[1:], lambda i: (li,) + zeros, pipeline_mode=pl.Buffered(1))
    return pl.BlockSpec((1,) + a.shape[1:], lambda i, j: (li,) + zeros, pipeline_mode=pl.Buffered(1))


def _inproj(x, mod, g1, w_mix, consts, geo, tm, li):
    d = x[0].shape[1]
    t_all = x[0].shape[0] + x[1].shape[0]
    nct = geo["t_ctx"] // tm
    tps = geo["r"] // tm
    nb = x[1].shape[0] // geo["r"]
    seq_of = lambda i: (i - nct) % nb
    pos_of = lambda i: (i - nct) // nb
    lat_tile = lambda i: seq_of(i) * tps + pos_of(i)

    def mod_row(i):
        return li * MOD_ROWS + jnp.where(i < nct, 0, 1 + seq_of(i))

    def rope_row(i):
        return jnp.where(i < nct, tps, pos_of(i))

    full = lambda a: pl.BlockSpec(a.shape, lambda i: (0,) * a.ndim, pipeline_mode=pl.Buffered(1))
    layer = lambda a: _layer_spec(a, li, 1)
    widths = [COL_A[1] - COL_A[0], COL_B[1] - COL_B[0], COL_C[1] - COL_C[0], COL_D[1] - COL_D[0]]
    tab = lambda w: pl.BlockSpec((tm, w), lambda i: (rope_row(i), 0))
    return pl.pallas_call(
        functools.partial(_inproj_kernel, nct),
        grid=(t_all // tm,),
        in_specs=[pl.BlockSpec((tm, d), lambda i: (jnp.minimum(i, nct - 1), 0)),
                  pl.BlockSpec((tm, d), lambda i: (jnp.where(i < nct, 0, lat_tile(i)), 0)),
                  pl.BlockSpec((1, 6, d), lambda i: (mod_row(i), 0, 0)),
                  layer(g1), layer(w_mix), full(consts["bda"]), full(consts["bdd"]),
                  layer(consts["ga"]), layer(consts["gb"]), layer(consts["gd"]),
                  tab(512), tab(512), tab(768), tab(768)],
        out_specs=[pl.BlockSpec((tm, w), lambda i: (jnp.where(i < nct, i, nct + lat_tile(i)), 0)) for w in widths],
        out_shape=[jax.ShapeDtypeStruct((t_all, w), F32) for w in widths],
        compiler_params=_params(("arbitrary",)),
    )(x[0], x[1], mod, g1, w_mix, consts["bda"], consts["bdd"], consts["ga"], consts["gb"], consts["gd"],
      consts["cosa"], consts["sina"], consts["cosd"], consts["sind"])


def _values_with_ones(v):
    lane = lax.broadcasted_iota(I32, (v.shape[0], LANES), 1)
    slabs = []
    for c in range(v.shape[1] // LANES):
        vs = v[:, c * LANES:(c + 1) * LANES]
        slabs.append(jnp.where(lane < HEAD_DIM, vs, 1.0))
        slabs.append(jnp.where(lane < HEAD_DIM, pltpu.roll(vs, HEAD_DIM, 1), 1.0))
    return jnp.concatenate(slabs, axis=1).astype(BF16)


def _attn_kernel(kind, geo, li, p_ref, ck_ref, cv_ref, sink_ref, lam_ref, gdo_ref, y_ref, kb_ref, vb_ref):
    r, past, tq, ng = geo["r"], geo["past"], geo["tq"], geo["n_ctx_groups"]
    lam_ref, gdo_ref = lam_ref.at[0], gdo_ref.at[0]
    s = pl.program_id(0)
    qi = pl.program_id(1)
    is_ctx = s < ng
    diff = kind == "D"
    kcols = (256, 512) if diff else (256, 384)
    vcols = (512, 768) if diff else (384, 512)
    wk = kcols[1] - kcols[0]
    dqk = DIFF_QK_DIM if diff else HEAD_DIM
    scale = 1.0 / math.sqrt(dqk)

    @pl.when(qi == 0)
    def _prep():
        kb_ref[0:r, :] = p_ref[:, kcols[0]:kcols[1]].astype(BF16)
        vb_ref[0:r, :] = _values_with_ones(p_ref[:, vcols[0]:vcols[1]])

        @pl.when(jnp.logical_not(is_ctx))
        def _cache():
            kb_ref[r:r + past, :] = ck_ref[0, 0].astype(BF16)
            vb_ref[r:r + past, :] = _values_with_ones(cv_ref[0, 0])

    n_sub = geo["tq_block"] // tq
    lane256 = lax.broadcasted_iota(I32, (tq, 256), 1)
    lane128 = lax.broadcasted_iota(I32, (tq, LANES), 1)

    def q_for(q, u):
        if diff:
            slab = q[:, (u // 4) * LANES:(u // 4 + 1) * LANES]
            lo = (u % 4) * DIFF_QK_DIM
            keep = (lane128 >= lo) & (lane128 < lo + DIFF_QK_DIM)
            return jnp.where(keep, slab, 0.0).astype(BF16), u // 4
        slab = q[:, (u // 2) * LANES:(u // 2 + 1) * LANES]
        g = u // 2
        if (u % 2) != g:
            slab = pltpu.roll(slab, HEAD_DIM, 1)
        keep = (lane128 < HEAD_DIM) if g == 0 else (lane128 >= HEAD_DIM)
        return jnp.where(keep, slab, 0.0).astype(BF16), 0

    def scores(q, u, srcs, sink):
        qm, kslab = q_for(q, u)
        ss = []
        for (st, n, mask) in srcs:
            kk = kb_ref[pl.ds(st, n), kslab * LANES:(kslab + 1) * LANES]
            sc = _dot_nt(qm, kk)
            if mask is not None:
                sc = jnp.where(mask, sc, -jnp.inf)
            ss.append(sc)
        mx = ss[0].max(axis=-1, keepdims=True)
        for sc in ss[1:]:
            mx = jnp.maximum(mx, sc.max(axis=-1, keepdims=True))
        if sink is not None:
            mx = jnp.maximum(mx, sink)
        return ss, mx

    def softmax_pv(ss, mx, vslab, srcs, sink, upper):
        ol = jnp.zeros((tq, LANES), F32)
        for (st, n, _), sc in zip(srcs, ss):
            pexp = jnp.exp((sc - mx).astype(BF16))
            ol = ol + _dot(pexp, vb_ref[pl.ds(st, n), vslab * LANES:(vslab + 1) * LANES])
        if sink is not None:
            ol = ol + jnp.where(lane128 >= HEAD_DIM, jnp.exp(sink - mx), 0.0)
        sw = pltpu.roll(ol, HEAD_DIM, 1)
        return sw / ol if upper else ol / sw

    if diff:
        lam_init = 0.8 - 0.6 * math.exp(-0.3 * li)
        lv = lam_ref[...]
        lam = (jnp.exp(jnp.sum(lv[0:1] * lv[1:2], axis=1, keepdims=True))
               - jnp.exp(jnp.sum(lv[2:3] * lv[3:4], axis=1, keepdims=True)) + lam_init)

    n_u = 8 if diff else 4
    sink_of = lambda u: sink_ref[li, u] if kind == "B" else None

    def finish(outs, j):
        halves = [outs[2 * h] - lam * outs[2 * h + 1] for h in range(4)] if diff else outs
        low = lane128 < HEAD_DIM
        acc = jnp.concatenate([jnp.where(low, halves[0], halves[1]), jnp.where(low, halves[2], halves[3])], axis=1)
        if diff:
            sq = acc * acc
            inv = jnp.zeros((tq, 256), F32)
            for h in range(4):
                head = (lane256 >= h * HEAD_DIM) & (lane256 < (h + 1) * HEAD_DIM)
                ms = jnp.sum(jnp.where(head, sq, 0.0), axis=-1, keepdims=True) * (1.0 / HEAD_DIM)
                inv = jnp.where(head, lax.rsqrt(ms + NORM_EPS), inv)
            acc = acc * inv * gdo_ref[...] * (1.0 - lam_init)
        y_ref[j * tq:(j + 1) * tq, :] = acc

    def run(jobs, batched):
        qs = [p_ref[pl.ds(row0, tq), 0:256] * scale for (row0, _) in jobs]
        items = [(j, u) for j in range(len(jobs)) for u in range(n_u)]
        ahead = len(items) if batched else 2
        issue = lambda j, u: scores(qs[j], u, jobs[j][1], sink_of(u))
        pending = [issue(*it) for it in items[:ahead]]
        outs = [[] for _ in jobs]
        for k, (j, u) in enumerate(items):
            if k + ahead < len(items):
                pending.append(issue(*items[k + ahead]))
            h = u // 2 if diff else u
            outs[j].append(softmax_pv(*pending.pop(0), h if diff else h // 2, jobs[j][1], sink_of(u), h % 2 == 1))
            if u == n_u - 1:
                finish(outs[j], j)

    base = qi * (n_sub * tq)
    rows0 = [pl.multiple_of(base + j * tq, tq) for j in range(n_sub)]

    @pl.when(is_ctx)
    def _ctx():
        run([(q0, [(q0, tq, None)]) for q0 in rows0], True)

    @pl.when(jnp.logical_not(is_ctx))
    def _lat():
        jobs = []
        for q0 in rows0:
            if kind == "B":
                span = tq + 2 * WINDOW
                st = pl.multiple_of(jnp.clip(q0 - WINDOW, 0, r - span), WINDOW)
                qpos = q0 + lax.broadcasted_iota(I32, (tq, span), 0)
                kpos = st + lax.broadcasted_iota(I32, (tq, span), 1)
                jobs.append((q0, [(st, span, jnp.abs(qpos - kpos) <= WINDOW), (r, past, None)]))
            else:
                jobs.append((q0, [(0, r + past, None)]))
        run(jobs, False)


def _attention(kind, geo, li, p, ck, cv, sink, lam, gdo):
    t_all, w = p.shape
    r, past, tq, ng = geo["r"], geo["past"], geo["tq_block"], geo["n_ctx_groups"]
    ns = t_all // r
    wk = 256 if kind == "D" else 128
    cache_idx = lambda s, q: (jnp.maximum(s - ng, 0), li, 0, 0)
    return pl.pallas_call(
        functools.partial(_attn_kernel, kind, geo, li),
        grid=(ns, r // tq),
        in_specs=[pl.BlockSpec((r, w), lambda s, q: (s, 0)),
                  pl.BlockSpec((1, 1, past, ck.shape[-1]), cache_idx),
                  pl.BlockSpec((1, 1, past, cv.shape[-1]), cache_idx),
                  pl.BlockSpec(memory_space=pltpu.SMEM),
                  _layer_spec(lam, li, 2), _layer_spec(gdo, li, 2)],
        out_specs=pl.BlockSpec((tq, 256), lambda s, q: (s * (r // tq) + q, 0)),
        out_shape=jax.ShapeDtypeStruct((t_all, 256), F32),
        scratch_shapes=[pltpu.VMEM((r + past, wk), BF16), pltpu.VMEM((r + past, 2 * wk), BF16)],
        compiler_params=_params(("arbitrary", "arbitrary")),
    )(p, ck, cv, sink, lam, gdo)


def _ssd_kernel(geo, li, p_ref, st0_ref, cw_ref, cb_ref, dtb_ref, alog_ref, dsk_ref, gs_ref,
                y_ref, so_ref, xpad, u_scr, dt_scr, la_scr, y_scr, upd_scr, cd_scr, ex_scr, st_scr, sf_ref, sb_ref):
    r, seq, ng = geo["r"], geo["seq"], geo["n_ctx_groups"]
    cw_ref, cb_ref, dtb_ref, alog_ref, dsk_ref, gs_ref = (
        t.at[0] for t in (cw_ref, cb_ref, dtb_ref, alog_ref, dsk_ref, gs_ref))
    q = SSM_CHUNK
    nh = SSM_HEADS
    is_ctx = pl.program_id(0) < ng
    inner = nh * SSM_HEAD_DIM
    conv_ch = inner + 2 * LANES
    nchunks = r // q
    spg = r // seq
    gap = 8

    def fill(n_seq, length):
        for j in range(n_seq + 1):
            xpad[j * (length + gap):j * (length + gap) + gap, :] = jnp.zeros((gap, conv_ch), F32)
        for j in range(n_seq):
            xpad[gap + j * (length + gap):gap + j * (length + gap) + length, :] = (
                p_ref[j * length:(j + 1) * length, inner:inner + conv_ch])

    @pl.when(is_ctx)
    def _():
        fill(spg, seq)

    @pl.when(jnp.logical_not(is_ctx))
    def _():
        fill(1, r)

    a_neg = -jnp.exp(alog_ref[...])
    seq_shift = jnp.where(is_ctx, int(math.log2(seq // q)), int(math.log2(r // q)))

    def pre(c, carry):
        r0 = pl.multiple_of(c * q, q)
        st = pl.multiple_of(r0 + lax.shift_right_logical(c, seq_shift) * gap, gap)
        big = xpad[pl.ds(st, q + 16), :]
        acc = jnp.zeros((q, conv_ch), F32) + cb_ref[...]
        for j in range(SSM_CONV):
            sh = pltpu.roll(big, (q + 16) - (gap - SSM_CONV // 2 + j), 0)[0:q]
            acc = acc + sh * cw_ref[j:j + 1, :]
        u_scr[pl.ds(r0, q), :] = _silu(acc)
        raw = p_ref[pl.ds(r0, q), inner + conv_ch:inner + conv_ch + LANES] + dtb_ref[...]
        dt = jnp.maximum(raw, 0.0) + jnp.log1p(jnp.exp(-jnp.abs(raw)))
        dt_scr[pl.ds(r0, q), :] = dt
        la_scr[pl.ds(r0, q), :] = dt * a_neg
        return carry

    lax.fori_loop(0, nchunks, pre, 0)

    row_i = lax.broadcasted_iota(I32, (q, q), 0)
    col_i = lax.broadcasted_iota(I32, (q, q), 1)
    keeps = (col_i <= row_i, col_i >= row_i)
    tris = tuple(jnp.where(k, 1.0, 0.0).astype(BF16) for k in keeps)
    lane128 = lax.broadcasted_iota(I32, (q, LANES), 1)
    lane256 = lax.broadcasted_iota(I32, (q, inner), 1)
    heads256 = [(lane256 >= h * SSM_HEAD_DIM) & (lane256 < (h + 1) * SSM_HEAD_DIM) for h in range(nh)]
    lane8 = lax.broadcasted_iota(I32, (8, inner), 1)
    gmask_t = (lax.broadcasted_iota(I32, (LANES, inner), 0) // SSM_STATE
               == lax.broadcasted_iota(I32, (LANES, inner), 1) // (2 * SSM_HEAD_DIM))

    def phase1(c, carry):
        r0 = pl.multiple_of(c * q, q)
        u = u_scr[pl.ds(r0, q), :]
        xs, bm, cm = u[:, 0:inner], u[:, inner:inner + LANES], u[:, inner + LANES:inner + 2 * LANES]
        bm_b = bm.astype(BF16)
        cb = [_dot_nt(jnp.where((lane128 // SSM_STATE) == g, cm, 0.0).astype(BF16), bm_b) for g in range(2)]
        a1, a2, a3 = _split3(la_scr[pl.ds(r0, q), :])
        cums = [_dot(t, a1) + _dot(t, a2) + _dot(t, a3) for t in tris]
        cum = jnp.where(lane128 < nh, cums[0], cums[1])
        pack = jnp.where(lane128 < 2 * nh, cum, pltpu.roll(dt_scr[pl.ds(r0, q), :], 2 * nh, 1))
        pack_t = pack.T
        bm_t = bm.T
        xstack = jnp.concatenate([jnp.where(heads256[h], xs, 0.0).astype(BF16) for h in range(nh)], axis=0)
        ms, bws = [], []
        for d in range(2):
            edge = q - 1 if d == 0 else 0
            m_h, bw_h, ecol, cdx = [], [], [], jnp.zeros((8, inner), F32)
            for h in range(nh):
                c_ = d * nh + h
                colb = jnp.broadcast_to(cum[:, c_:c_ + 1], (q, q))
                crow = pack_t[c_:c_ + 1, :]
                dtrow = pack_t[2 * nh + c_:2 * nh + c_ + 1, :]
                lm = jnp.exp(jnp.where(keeps[d], colb - crow, -jnp.inf))
                m_h.append((cb[h // 2] * lm * dtrow).astype(BF16))
                tot = crow[:, edge:edge + 1]
                bw_h.append((bm_t * (dtrow * jnp.exp(tot - crow))).astype(BF16))
                ecol.append(jnp.exp(colb))
                cdx = jnp.where((lane8 >= h * SSM_HEAD_DIM) & (lane8 < (h + 1) * SSM_HEAD_DIM), jnp.exp(tot), cdx)
            ms.append(jnp.concatenate(m_h, axis=1))
            bws.append(jnp.concatenate(bw_h, axis=1))
            ex_scr[d, pl.ds(r0, q), :] = jnp.concatenate(
                [jnp.where(lane128 < SSM_HEAD_DIM, ecol[0], ecol[1]),
                 jnp.where(lane128 < SSM_HEAD_DIM, ecol[2], ecol[3])], axis=1)
            cd_scr[d, c] = cdx
        res = _dot(jnp.concatenate(ms + bws, axis=0), xstack)
        y_scr[pl.ds(r0, q), :] = res[0:q] + res[q:2 * q]
        upd_scr[0, c] = jnp.where(gmask_t, res[2 * q:3 * q], 0.0)
        upd_scr[1, c] = jnp.where(gmask_t, res[3 * q:4 * q], 0.0)
        return carry

    lax.fori_loop(0, nchunks, phase1, 0, unroll=2)

    srefs = (sf_ref, sb_ref)

    def scan_step(d, c):
        s_in = srefs[d][...]
        st_scr[d, c] = s_in.astype(BF16)
        srefs[d][...] = s_in * cd_scr[d, c][0:1, :] + upd_scr[d, c]

    @pl.when(is_ctx)
    def _ctx():
        cps = seq // q

        def one_seq(j, carry):
            for d in range(2):
                srefs[d][...] = jnp.zeros((LANES, inner), F32)
                for ci in range(cps):
                    scan_step(d, j * cps + (ci if d == 0 else cps - 1 - ci))
                sv = srefs[d][...]
                so_ref[j, d, 0:LANES, :] = sv[:, 0:LANES].T[:, 0:SSM_STATE]
                so_ref[j, d, LANES:inner, :] = pltpu.roll(sv[:, LANES:inner].T, SSM_STATE, 1)[:, 0:SSM_STATE]
            return carry

        lax.fori_loop(0, spg, one_seq, 0)

    @pl.when(jnp.logical_not(is_ctx))
    def _lat():
        for d in range(2):
            srefs[d][...] = st0_ref[0, 0, d]

        def body(i, carry):
            scan_step(0, i)
            scan_step(1, nchunks - 1 - i)
            return carry

        lax.fori_loop(0, nchunks, body, 0)

    def fin(c, carry):
        r0 = pl.multiple_of(c * q, q)
        cm_b = u_scr[pl.ds(r0, q), inner + LANES:inner + 2 * LANES].astype(BF16)
        y = y_scr[pl.ds(r0, q), :] + u_scr[pl.ds(r0, q), 0:inner] * dsk_ref[...]
        for d in range(2):
            y = y + _dot(cm_b, st_scr[d, c]) * ex_scr[d, pl.ds(r0, q), :]
        y = y * _silu(p_ref[pl.ds(r0, q), 0:inner])
        y_ref[pl.ds(r0, q), :] = _rms(y, gs_ref[...])
        return carry

    lax.fori_loop(0, nchunks, fin, 0, unroll=4)


def _ssd(geo, li, pc, st0, cw, cb, dtb, alog, dsk, gs):
    t_all, w = pc.shape
    r, seq, ng = geo["r"], geo["seq"], geo["n_ctx_groups"]
    ns = t_all // r
    spg = r // seq
    nchunks = r // SSM_CHUNK
    inner = SSM_HEADS * SSM_HEAD_DIM
    full = lambda a: _layer_spec(a, li, 1)
    return pl.pallas_call(
        functools.partial(_ssd_kernel, geo, li),
        grid=(ns,),
        in_specs=[pl.BlockSpec((r, w), lambda s: (s, 0)),
                  pl.BlockSpec((1, 1, 2, LANES, inner), lambda s: (jnp.maximum(s - ng, 0), li, 0, 0, 0)),
                  full(cw), full(cb), full(dtb), full(alog), full(dsk), full(gs)],
        out_specs=[pl.BlockSpec((r, inner), lambda s: (s, 0)),
                   pl.BlockSpec((spg, 2, inner, SSM_STATE), lambda s: (jnp.minimum(s, ng - 1), 0, 0, 0))],
        out_shape=[jax.ShapeDtypeStruct((t_all, inner), F32),
                   jax.ShapeDtypeStruct((ng * spg, 2, inner, SSM_STATE), F32)],
        scratch_shapes=[pltpu.VMEM((r + 8 * (spg + 1), 2 * inner), F32), pltpu.VMEM((r, 2 * inner), F32),
                        pltpu.VMEM((r, LANES), F32), pltpu.VMEM((r, LANES), F32),
                        pltpu.VMEM((r, inner), F32),
                        pltpu.VMEM((2, nchunks, LANES, inner), F32), pltpu.VMEM((2, nchunks, 8, inner), F32),
                        pltpu.VMEM((2, r, inner), F32), pltpu.VMEM((2, nchunks, LANES, inner), BF16),
                        pltpu.VMEM((LANES, inner), F32), pltpu.VMEM((LANES, inner), F32)],
        compiler_params=_params(("arbitrary",)),
    )(pc, st0, cw, cb, dtb, alog, dsk, gs)


def _merge_kernel(nct, xc_ref, xl_ref, mod_ref, g1_ref, g2_ref, ya_ref, yb_ref, yc_ref, yd_ref, wg_ref, wbr_ref,
                  wo_ref, wr_ref, x1_ref, h2_ref, aff_ref):
    g1_ref, g2_ref, wg_ref, wbr_ref, wo_ref, wr_ref = (
        t.at[0] for t in (g1_ref, g2_ref, wg_ref, wbr_ref, wo_ref, wr_ref))
    tm, d = xc_ref.shape
    is_ctx = pl.program_id(0) < nct
    m = mod_ref[0]
    w_hi, w_lo = _split2(wr_ref[...])
    n_sub = 2
    ts = tm // n_sub

    def mix(i):
        rows = slice(i * ts, (i + 1) * ts)
        x = jnp.where(is_ctx, xc_ref[rows, :], xl_ref[rows, :])
        hb = (_rms(x, g1_ref[...]) * (1.0 + m[1:2]) + m[0:1]).astype(BF16)
        merged = jnp.zeros(x.shape, F32)
        for n, y_ref in enumerate((ya_ref, yb_ref, yc_ref, yd_ref)):
            gate = jax.nn.sigmoid(_dot(hb, wg_ref[:, n * d:(n + 1) * d]))
            merged = merged + gate * _dot(y_ref[rows, :].astype(BF16), wbr_ref[n])
        return x + m[2:3] * _dot(merged.astype(BF16), wo_ref[...])

    def tail(i, x1):
        rows = slice(i * ts, (i + 1) * ts)
        x1_ref[rows, :] = x1
        h2 = _rms(x1, g2_ref[...]) * (1.0 + m[4:5]) + m[3:4]
        h2_ref[rows, :] = h2.astype(BF16)
        h_hi, h_lo = _split2(h2)
        logit = _dot_nt(w_hi, h_hi) + _dot_nt(w_hi, h_lo) + _dot_nt(w_lo, h_hi)
        e = jnp.exp(logit - logit.max(axis=0, keepdims=True))
        aff_ref[:, rows] = e / e.sum(axis=0, keepdims=True)

    x1_prev = mix(0)
    for i in range(1, n_sub):
        x1_next = mix(i)
        tail(i - 1, x1_prev)
        x1_prev = x1_next
    tail(n_sub - 1, x1_prev)


def _merge(x, mod, g1, g2, ys, wg, wbr, wo, wr_t, geo, tm, li):
    d = x[0].shape[1]
    t_all = x[0].shape[0] + x[1].shape[0]
    ne = wr_t.shape[1]
    nct = geo["t_ctx"] // tm
    tps = geo["r"] // tm
    mod_row = lambda i: li * MOD_ROWS + jnp.where(i < nct, 0, 1 + (i - nct) // tps)
    full = lambda a: _layer_spec(a, li, 1)
    ytile = pl.BlockSpec((tm, BRANCH_W), lambda i: (i, 0))
    return pl.pallas_call(
        functools.partial(_merge_kernel, nct),
        grid=(t_all // tm,),
        in_specs=[pl.BlockSpec((tm, d), lambda i: (jnp.minimum(i, nct - 1), 0)),
                  pl.BlockSpec((tm, d), lambda i: (jnp.maximum(i - nct, 0), 0)),
                  pl.BlockSpec((1, 6, d), lambda i: (mod_row(i), 0, 0)),
                  full(g1), full(g2), ytile, ytile, ytile, ytile, full(wg), full(wbr), full(wo), full(wr_t)],
        out_specs=[pl.BlockSpec((tm, d), lambda i: (i, 0)),
                   pl.BlockSpec((tm, d), lambda i: (i, 0)),
                   pl.BlockSpec((ne, tm), lambda i: (0, i))],
        out_shape=[jax.ShapeDtypeStruct((t_all, d), F32),
                   jax.ShapeDtypeStruct((t_all, d), BF16),
                   jax.ShapeDtypeStruct((ne, t_all), F32)],
        compiler_params=_params(("arbitrary",)),
    )(x[0], x[1], mod, g1, g2, *ys, wg, wbr, wo, wr_t)


def _kth_largest_bits(a, cap):
    def keeps(cand):
        cnt = jnp.sum(jnp.where(a >= lax.bitcast_convert_type(cand, F32), 1.0, 0.0), axis=1, keepdims=True)
        return cnt >= cap

    def body(i, thr):
        lo = 29 - 2 * i
        c1, c2, c3 = (thr | lax.shift_left(jnp.int32(k), lo) for k in (1, 2, 3))
        return jnp.where(keeps(c3), c3, jnp.where(keeps(c2), c2, jnp.where(keeps(c1), c1, thr)))

    thr = lax.fori_loop(0, 15, body, jnp.zeros((a.shape[0], 1), I32))
    return jnp.where(keeps(thr | 1), thr | 1, thr)


def _route_kernel(geo, aff_ref, slot_ref, start_ref):
    r, seq, ng = geo["r"], geo["seq"], geo["n_ctx_groups"]
    ne = aff_ref.shape[0]
    blk = ROUTE_TILE
    s = pl.program_id(0)
    lane_s = lax.broadcasted_iota(I32, (ne, LANES), 1)
    upper = jnp.where(lax.broadcasted_iota(I32, (blk, blk), 0) < lax.broadcasted_iota(I32, (blk, blk), 1),
                      1.0, 0.0).astype(BF16)

    def select(a, cap, nblk):
        thr = _kth_largest_bits(a, cap)
        gt = a >= lax.bitcast_convert_type(thr + 1, F32)
        eq = (a >= lax.bitcast_convert_type(thr, F32)) & jnp.logical_not(gt)
        need = cap - jnp.sum(jnp.where(gt, 1.0, 0.0), axis=1, keepdims=True)
        eq_f = jnp.where(eq, 1.0, 0.0)
        outs, firsts = [], []
        carry_e = jnp.zeros((a.shape[0], 1), F32)
        carry_s = jnp.zeros((a.shape[0], 1), F32)
        for b in range(nblk):
            sl = slice(b * blk, (b + 1) * blk)
            pe = _dot(eq_f[:, sl].astype(BF16), upper) + carry_e
            sel = gt[:, sl] | (eq[:, sl] & (pe < need))
            sel_f = jnp.where(sel, 1.0, 0.0)
            ps = _dot(sel_f.astype(BF16), upper) + carry_s
            outs.append(jnp.where(sel, ps, -1.0))
            firsts.append(carry_s)
            carry_e = carry_e + eq_f[:, sl].sum(axis=1, keepdims=True)
            carry_s = carry_s + sel_f.sum(axis=1, keepdims=True)
        return outs, firsts

    @pl.when(s < ng)
    def _ctx():
        spg = r // seq
        cap = EC_FACTOR * seq // ne
        a = jnp.concatenate([aff_ref[:, j * seq:(j + 1) * seq] for j in range(spg)], axis=0)
        out, _ = select(a, cap, 1)
        for j in range(spg):
            o = out[0][j * ne:(j + 1) * ne]
            slot_ref[:, j * seq:(j + 1) * seq] = jnp.where(o >= 0, o + j * cap, -1.0).astype(I32)
        start_ref[...] = jnp.minimum(lane_s, spg) * cap

    @pl.when(s >= ng)
    def _lat():
        cap = EC_FACTOR * r // ne
        out, firsts = select(aff_ref[...], cap, r // blk)
        st = jnp.full((ne, LANES), float(cap), F32)
        for b, o in enumerate(out):
            slot_ref[:, b * blk:(b + 1) * blk] = o.astype(I32)
            st = jnp.where(lane_s == b, firsts[b], st)
        start_ref[...] = st.astype(I32)


def _route(aff_t, geo):
    ne, t_all = aff_t.shape
    r = geo["r"]
    return pl.pallas_call(
        functools.partial(_route_kernel, geo),
        grid=(t_all // r,),
        in_specs=[pl.BlockSpec((ne, r), lambda s: (0, s))],
        out_specs=[pl.BlockSpec((ne, r), lambda s: (0, s)), pl.BlockSpec((ne, LANES), lambda s: (0, s))],
        out_shape=[jax.ShapeDtypeStruct((ne, t_all), I32), jax.ShapeDtypeStruct((ne, (t_all // r) * LANES), I32)],
        compiler_params=_params(("arbitrary",)),
    )(aff_t)


def _tile_starts(starts_ref, e, s, ns, nt):
    base = (e * ns + s) * (nt + 1)
    return [starts_ref[base + i] for i in range(nt + 1)]


def _expert_kernel(starts_ref, x_ref, slot_ref, aff_ref, w1_ref, w3_ref, w2_ref, yo_ref, wb1, wb3, wb2,
                   xs_scr, gs_scr):
    e, s, ns = pl.program_id(0), pl.program_id(1), pl.num_programs(1)

    @pl.when(s == 0)
    def _cast():
        wb1[...] = w1_ref[0, 0].astype(BF16)
        wb3[...] = w3_ref[0, 0].astype(BF16)
        wb2[...] = w2_ref[0, 0].astype(BF16)

    cap = yo_ref.shape[2]
    r = x_ref.shape[0]
    nt = r // ROUTE_TILE
    win = min(GATHER_WINDOW, cap)
    slot = slot_ref[0]
    aff = aff_ref[0]
    a = _tile_starts(starts_ref, e, s, ns, nt)
    fits = a[1] - a[0] <= win - 16
    for i in range(1, nt):
        fits = fits & (a[i + 1] - a[i] <= win - 16)

    @pl.when(fits)
    def _windows():
        xs_scr[...] = jnp.zeros(xs_scr.shape, F32)
        gs_scr[...] = jnp.zeros(gs_scr.shape, F32)
        for i in range(nt):
            cols = slice(i * ROUTE_TILE, (i + 1) * ROUTE_TILE)
            w0 = pl.multiple_of(jnp.minimum((a[i] // 16) * 16, cap - win), 16)
            pick = (lax.broadcasted_iota(I32, (win, ROUTE_TILE), 0) + w0) == slot[:, cols]
            xs_scr[pl.ds(w0, win), :] += _dot(jnp.where(pick, 1.0, 0.0).astype(BF16), x_ref[cols, :])
            gs_scr[pl.ds(w0, win), :] += jnp.sum(jnp.where(pick, aff[:, cols], 0.0), axis=1, keepdims=True)

    @pl.when(jnp.logical_not(fits))
    def _all_rows():
        pick = lax.broadcasted_iota(I32, (cap, r), 0) == slot
        xs_scr[...] = _dot(jnp.where(pick, 1.0, 0.0).astype(BF16), x_ref[...])
        gs_scr[...] = jnp.broadcast_to(jnp.sum(jnp.where(pick, aff, 0.0), axis=1, keepdims=True), gs_scr.shape)

    xs = xs_scr[...].astype(BF16)
    hid = _silu(_dot(xs, wb1[...])) * _dot(xs, wb3[...])
    yo_ref[0, 0] = (_dot(hid.astype(BF16), wb2[...]) * gs_scr[:, 0:1]).astype(BF16)


def _experts(starts, h2, slot3, aff3, w1, w3, w2, geo, li):
    t_all, d = h2.shape
    _, ne, _, f = w1.shape
    r = geo["r"]
    ns = t_all // r
    cap = EC_FACTOR * r // ne
    wspec = lambda a: pl.BlockSpec((1, 1) + a.shape[2:], lambda e, s, st: (li, e, 0, 0))
    return pl.pallas_call(
        _expert_kernel,
        grid_spec=pltpu.PrefetchScalarGridSpec(
            num_scalar_prefetch=1,
            grid=(ne, ns),
            in_specs=[pl.BlockSpec((r, d), lambda e, s, st: (s, 0)),
                      pl.BlockSpec((1, 1, r), lambda e, s, st: (e, 0, s)),
                      pl.BlockSpec((1, 1, r), lambda e, s, st: (e, 0, s)),
                      wspec(w1), wspec(w3), wspec(w2)],
            out_specs=pl.BlockSpec((1, 1, cap, d), lambda e, s, st: (s, e, 0, 0)),
            scratch_shapes=[pltpu.VMEM((d, f), BF16), pltpu.VMEM((d, f), BF16), pltpu.VMEM((f, d), BF16),
                            pltpu.VMEM((cap, d), F32), pltpu.VMEM((cap, LANES), F32)]),
        out_shape=jax.ShapeDtypeStruct((ns, ne, cap, d), BF16),
        compiler_params=_params(("arbitrary", "arbitrary")),
    )(starts, h2, slot3, aff3, w1, w3, w2)


def _scatter_kernel(ng, starts_ref, x_ref, mod_ref, slot_ref, yo_ref, oc_ref, ol_ref, ywin):
    tt = x_ref.shape[0]
    _, ne, cap, d = yo_ref.shape
    s, i, ns, nt = pl.program_id(0), pl.program_id(1), pl.num_programs(0), pl.num_programs(1)
    win = min(SCATTER_WINDOW, cap)
    slot = slot_ref[...]
    first = [starts_ref[(e * ns + s) * (nt + 1) + i] for e in range(ne)]
    last = [starts_ref[(e * ns + s) * (nt + 1) + i + 1] for e in range(ne)]
    fits = last[0] - first[0] <= win - 16
    for e in range(1, ne):
        fits = fits & (last[e] - first[e] <= win - 16)

    is_ctx = s < ng

    def emit(ffn):
        out = x_ref[...] + mod_ref[0][5:6] * ffn

        @pl.when(is_ctx)
        def _():
            oc_ref[...] = out

        @pl.when(jnp.logical_not(is_ctx))
        def _():
            ol_ref[...] = out

    @pl.when(fits)
    def _windows():
        lane = lax.broadcasted_iota(I32, (tt, win), 1)
        hot = []
        for e in range(ne):
            w0 = pl.multiple_of(jnp.minimum((first[e] // 16) * 16, cap - win), 16)
            ywin[e * win:(e + 1) * win, :] = yo_ref[0, e, pl.ds(w0, win), :]
            hot.append(jnp.where(slot[:, e:e + 1] - w0 == lane, 1.0, 0.0).astype(BF16))
        emit(_dot(jnp.concatenate(hot, axis=1), ywin[...]))

    @pl.when(jnp.logical_not(fits))
    def _all_rows():
        lane = lax.broadcasted_iota(I32, (tt, cap), 1)
        hot = jnp.concatenate([jnp.where(slot[:, e:e + 1] == lane, 1.0, 0.0).astype(BF16) for e in range(ne)], axis=1)
        emit(_dot(hot, yo_ref[0].reshape(ne * cap, d)))


def _scatter(starts, x1, mod, slot_t, yo, geo, li):
    t_all, d = x1.shape
    ns, ne, cap, _ = yo.shape
    r = geo["r"]
    ng = geo["n_ctx_groups"]
    tt = ROUTE_TILE
    tpg = r // tt
    nct = ng * tpg
    win = min(SCATTER_WINDOW, cap)
    mod_row = lambda s, i, st: li * MOD_ROWS + jnp.where(s < ng, 0, 1 + s - ng)
    out_specs = [pl.BlockSpec((tt, d), lambda s, i, st: (jnp.minimum(s * tpg + i, nct - 1), 0)),
                 pl.BlockSpec((tt, d), lambda s, i, st: (jnp.maximum(s * tpg + i - nct, 0), 0))]
    out_shape = [jax.ShapeDtypeStruct((nct * tt, d), F32), jax.ShapeDtypeStruct((t_all - nct * tt, d), F32)]
    return pl.pallas_call(
        functools.partial(_scatter_kernel, ng),
        grid_spec=pltpu.PrefetchScalarGridSpec(
            num_scalar_prefetch=1,
            grid=(ns, tpg),
            in_specs=[pl.BlockSpec((tt, d), lambda s, i, st: (s * tpg + i, 0)),
                      pl.BlockSpec((1, 6, d), lambda s, i, st: (mod_row(s, i, st), 0, 0)),
                      pl.BlockSpec((tt, ne), lambda s, i, st: (s * tpg + i, 0)),
                      pl.BlockSpec((1, ne, cap, d), lambda s, i, st: (s, 0, 0, 0))],
            out_specs=out_specs,
            scratch_shapes=[pltpu.VMEM((ne * win, d), BF16)]),
        out_shape=out_shape,
        compiler_params=_params(("arbitrary", "arbitrary")),
    )(starts, x1, mod, slot_t, yo)


def _rope_table(s, d, n_qk_lanes, n_v_lanes, pad_rows):
    n = d // 4
    t = np.arange(s)
    rows = (t // GRID_W).astype(np.float64)
    cols = (t % GRID_W).astype(np.float64)
    inv = ROPE_THETA ** (-np.arange(n, dtype=np.float64) / n)
    ang = np.stack([rows[:, None] * inv, cols[:, None] * inv], axis=1)
    cos, sin = np.cos(ang), np.sin(ang)
    cos_h = np.concatenate([cos[:, 0], cos[:, 0], cos[:, 1], cos[:, 1]], axis=-1)
    sin_h = np.concatenate([-sin[:, 0], sin[:, 0], -sin[:, 1], sin[:, 1]], axis=-1)
    reps = n_qk_lanes // d
    cos_f = np.concatenate([np.tile(cos_h, (1, reps)), np.ones((s, n_v_lanes))], axis=1)
    sin_f = np.concatenate([np.tile(sin_h, (1, reps)), np.zeros((s, n_v_lanes))], axis=1)
    w = n_qk_lanes + n_v_lanes
    cos_f = np.concatenate([cos_f, np.ones((pad_rows, w))], axis=0)
    sin_f = np.concatenate([sin_f, np.zeros((pad_rows, w))], axis=0)
    return jnp.asarray(cos_f.astype(np.float32)), jnp.asarray(sin_f.astype(np.float32))


def _block_diag_ones(w, gsize):
    i = np.arange(w)
    return jnp.asarray((i[:, None] // gsize == i[None, :] // gsize).astype(np.float32), dtype=BF16)


def kernel(x_prompt, x_sample, c, cache_a_k, cache_a_v, cache_b_k, cache_b_v, state_ssm, cache_d_k, cache_d_v, c_ctx, w_ada, b_ada, g_norm1, g_norm2, w_in, g_qa, g_ka, g_qb, g_kb, sink_b, conv_w, conv_b, dt_bias, a_log, d_skip, g_ssm, g_qd, g_kd, lam_q1, lam_k1, lam_q2, lam_k2, g_dout, w_br, w_out, w_router, w_e1, w_e3, w_e2):
    batch, seq, d = x_prompt.shape
    dec_batch, dec_seq, _ = x_sample.shape
    depth = w_in.shape[0]
    past = cache_a_k.shape[2]
    ne = w_router.shape[2]
    r = dec_seq
    t_ctx = batch * seq
    assert t_ctx % r == 0 and r % seq == 0 and seq == 256 and past % 16 == 0
    tm = 512
    assert t_ctx % tm == 0 and r % tm == 0
    geo = dict(r=r, seq=seq, past=past, tq=256, tq_block=512, t_ctx=t_ctx, n_ctx_groups=t_ctx // r)
    ng = geo["n_ctx_groups"]

    cosa, sina = _rope_table(r, HEAD_DIM, 384, 128, tm)
    cosd, sind = _rope_table(r, DIFF_QK_DIM, 512, 256, tm)
    bda = _block_diag_ones(256, HEAD_DIM)
    bdd = _block_diag_ones(256, DIFF_QK_DIM)
    inner = SSM_HEADS * SSM_HEAD_DIM
    gsel = (np.arange(LANES)[:, None] // SSM_STATE) == (np.arange(inner)[None, :] // (2 * SSM_HEAD_DIM))

    cvec = jnp.concatenate([c_ctx[None, :], c, jnp.zeros((MOD_ROWS - 1 - dec_batch, d), F32)], axis=0)
    mod_all = _adaln(cvec, w_ada, b_ada)

    x = (x_prompt.reshape(t_ctx, d), x_sample.reshape(dec_batch * r, d))
    ck_a = cache_a_k.reshape(dec_batch, depth, past, -1)
    cv_a = cache_a_v.reshape(dec_batch, depth, past, -1)
    ck_b = cache_b_k.reshape(dec_batch, depth, past, -1)
    cv_b = cache_b_v.reshape(dec_batch, depth, past, -1)
    ck_d = cache_d_k.reshape(dec_batch, depth, past, -1)
    cv_d = cache_d_v.reshape(dec_batch, depth, past, -1)
    st0 = jnp.moveaxis(state_ssm, -1, -3).reshape(dec_batch, depth, 2, SSM_STATE, inner)
    st0 = jnp.where(gsel, jnp.concatenate([st0, st0], axis=-2), 0.0)

    mod = mod_all.reshape(depth * MOD_ROWS, 6, d)
    w_mix, w_gate = _split_w_in(w_in)
    w_br_b, w_out_b = w_br.astype(BF16), w_out.astype(BF16)
    w_r_t = jnp.swapaxes(w_router, 1, 2)
    lay = lambda *parts: jnp.concatenate(parts, axis=-1)[:, None, :]
    rep = lambda g, n: jnp.tile(g, (1, n))
    consts = dict(
        bda=bda, bdd=bdd, cosa=cosa, sina=sina, cosd=cosd, sind=sind,
        ga=lay(rep(g_qa, 4), rep(g_ka, 2), jnp.ones((depth, 128), F32)),
        gb=lay(rep(g_qb, 4), rep(g_kb, 2), jnp.ones((depth, 128), F32)),
        gd=lay(rep(g_qd, 8), rep(g_kd, 8), jnp.ones((depth, 256), F32)))
    g1, g2 = g_norm1[:, None, :], g_norm2[:, None, :]
    lam = jnp.stack([lam_q1, lam_k1, lam_q2, lam_k2], axis=1)
    gdo = rep(g_dout, 4)[:, None, :]
    pad3 = lambda a, rows, lanes: jnp.pad(a, ((0, 0), (0, rows - a.shape[1]), (0, lanes - a.shape[2])))
    ssd_par = (pad3(conv_w, 8, conv_w.shape[2]), conv_b[:, None, :],
               pad3(dt_bias.reshape(depth, 1, -1), 1, LANES), pad3(a_log.reshape(depth, 1, -1), 1, LANES),
               jnp.repeat(d_skip, SSM_HEAD_DIM, axis=1)[:, None, :], g_ssm[:, None, :])

    new = {k: [] for k in ("a_k", "a_v", "b_k", "b_v", "ssm", "d_k", "d_v")}
    for li in range(depth):
        pa, pb, pc, pd = _inproj(x, mod, g1, w_mix, consts, geo, tm, li)
        ya = _attention("A", geo, li, pa, ck_a, cv_a, sink_b, lam, gdo)
        yb = _attention("B", geo, li, pb, ck_b, cv_b, sink_b, lam, gdo)
        yd = _attention("D", geo, li, pd, ck_d, cv_d, sink_b, lam, gdo)
        yc, ssm_new = _ssd(geo, li, pc, st0, *ssd_par)

        x1, h2, aff_t = _merge(x, mod, g1, g2, (ya, yb, yc, yd), w_gate, w_br_b, w_out_b, w_r_t, geo, tm, li)
        slot, starts = _route(aff_t, geo)
        starts = starts.reshape(ne, -1, LANES)[:, :, :r // ROUTE_TILE + 1].reshape(-1)
        yo = _experts(starts, h2, slot.reshape(ne, 1, -1), aff_t.reshape(ne, 1, -1), w_e1, w_e3, w_e2, geo, li)
        x = _scatter(starts, x1, mod, slot.T, yo, geo, li)

        ctx = lambda p, lo, hi: p[:t_ctx, lo:hi]
        new["a_k"].append(ctx(pa, 256, 384).reshape(batch, seq, 2, HEAD_DIM))
        new["a_v"].append(ctx(pa, 384, 512).reshape(batch, seq, 2, HEAD_DIM))
        new["b_k"].append(ctx(pb, 256, 384).reshape(batch, seq, 2, HEAD_DIM))
        new["b_v"].append(ctx(pb, 384, 512).reshape(batch, seq, 2, HEAD_DIM))
        new["d_k"].append(ctx(pd, 256, 512).reshape(batch, seq, 4, 2, DIFF_QK_DIM))
        new["d_v"].append(ctx(pd, 512, 768).reshape(batch, seq, 4, HEAD_DIM))
        new["ssm"].append(ssm_new.reshape(batch, 2, SSM_HEADS, SSM_HEAD_DIM, SSM_STATE))

    st = lambda k: jnp.stack(new[k], axis=1)
    return (x[0].reshape(batch, seq, d), x[1].reshape(dec_batch, r, d),
            st("a_k"), st("a_v"), st("b_k"), st("b_v"), st("ssm"), st("d_k"), st("d_v"))
```
